```python
import jax, jax.numpy as jnp
from jax import lax
import numpy as np

D_MODEL = 2048
BATCH = 4
SEQ = 4096
DEPTH = 4

CHUNK = 64

FOX_WIDTH = D_MODEL // 2
FOX_HEAD_DIM = 128
FOX_HEADS = FOX_WIDTH // FOX_HEAD_DIM
Q_BLOCK = 128
FORGET_BIAS_MEAN = 2.0

SGU_WIDTH = D_MODEL // 2
SGU_GROUP_DIM = 128
SGU_GROUPS = SGU_WIDTH // SGU_GROUP_DIM
SGU_CHUNK = 128

D_FF = 5632
CONV_WIDTH = 3

RMS_EPS = 1e-6

IN_WIDTH = 3 * FOX_WIDTH + FOX_HEADS + 2 * SGU_WIDTH + 2 * D_MODEL

kernel_name = "hybrid_fox_sgu_convffn_trunk"


def rmsnorm(x, gain):
    xf = x.astype(jnp.float32)
    inv = lax.rsqrt(jnp.mean(xf * xf, axis=-1, keepdims=True) + RMS_EPS)
    return (xf * inv).astype(x.dtype) * gain


def split_in_proj(proj):
    sizes = (FOX_WIDTH, FOX_WIDTH, FOX_WIDTH, FOX_HEADS,
             SGU_WIDTH, SGU_WIDTH, D_MODEL, D_MODEL)
    points = tuple(int(p) for p in np.cumsum(sizes)[:-1])
    return jnp.split(proj, points, axis=-1)


def forgetting_attention(q, k, v, log_f):
    B, S, H, Dh = q.shape
    n_blk = S // Q_BLOCK
    scale = Dh ** -0.5
    c = jnp.cumsum(log_f, axis=1).transpose(0, 2, 1)
    qb = q.reshape(B, n_blk, Q_BLOCK, H, Dh).transpose(1, 0, 2, 3, 4)
    cb = c.reshape(B, H, n_blk, Q_BLOCK).transpose(2, 0, 1, 3)
    kpos = jnp.arange(S)

    def block(args):
        i, qi, ci = args
        s = jnp.einsum('bqhd,bkhd->bhqk', qi, k,
                       preferred_element_type=jnp.float32) * scale
        s = s + ci[..., :, None] - c[:, :, None, :]
        qpos = i * Q_BLOCK + jnp.arange(Q_BLOCK)
        mask = kpos[None, :] <= qpos[:, None]
        s = jnp.where(mask, s, -1e30)
        p = jax.nn.softmax(s, axis=-1).astype(v.dtype)
        return jnp.einsum('bhqk,bkhd->bqhd', p, v)

    o = lax.map(block, (jnp.arange(n_blk), qb, cb))
    return o.transpose(1, 0, 2, 3, 4).reshape(B, S, H * Dh)


def spatial_gating(u, v, g_norm, w_s, b_s):
    B, S, _ = v.shape
    n = S // SGU_CHUNK
    vc = rmsnorm(v, g_norm).reshape(B, n, SGU_CHUNK, SGU_GROUPS, SGU_GROUP_DIM)
    w = jnp.tril(w_s)
    mixed = jnp.einsum('gts,bnsgc->bntgc', w, vc) + b_s.T[None, None, :, :, None]
    return u * mixed.reshape(B, S, SGU_WIDTH)


def conv_ffn(h, w_up, conv_w, conv_b, w_down):
    S = h.shape[1]
    a, b = jnp.split(h @ w_up, 2, axis=-1)
    a_pad = jnp.pad(a, ((0, 0), (CONV_WIDTH - 1, 0), (0, 0)))
    acc = conv_b + conv_w[0] * a_pad[:, 0:S]
    for tap in range(1, CONV_WIDTH):
        acc = acc + conv_w[tap] * a_pad[:, tap:tap + S]
    return (jax.nn.gelu(acc) * b) @ w_down


def setup_inputs(seed: int = 0) -> dict:
    key = jax.random.key(seed)
    ks = jax.random.split(key, 16)
    f32 = jnp.float32

    def nrm(k, shape, fan_in):
        return jax.random.normal(k, shape, f32) * (fan_in ** -0.5)

    def gain(k, shape):
        return 1.0 + 0.01 * jax.random.normal(k, shape, f32)

    return {
        "x": jax.random.normal(ks[0], (BATCH, SEQ, D_MODEL), f32),
        "g_mix": gain(ks[1], (DEPTH, D_MODEL)),
        "w_in": nrm(ks[2], (DEPTH, D_MODEL, IN_WIDTH), D_MODEL),
        "b_forget": FORGET_BIAS_MEAN + 0.5 * jax.random.normal(ks[3], (DEPTH, FOX_HEADS), f32),
        "g_sgu": gain(ks[4], (DEPTH, SGU_WIDTH)),
        "w_spatial": nrm(ks[5], (DEPTH, SGU_GROUPS, SGU_CHUNK, SGU_CHUNK), SGU_CHUNK),
        "b_spatial": 1.0 + 0.1 * jax.random.normal(ks[6], (DEPTH, SGU_GROUPS, SGU_CHUNK), f32),
        "w_branch_a": nrm(ks[7], (DEPTH, FOX_WIDTH, D_MODEL), FOX_WIDTH),
        "w_branch_b": nrm(ks[8], (DEPTH, SGU_WIDTH, D_MODEL), SGU_WIDTH),
        "w_out": nrm(ks[9], (DEPTH, D_MODEL, D_MODEL), D_MODEL),
        "g_ffn": gain(ks[10], (DEPTH, D_MODEL)),
        "w_up": nrm(ks[11], (DEPTH, D_MODEL, 2 * D_FF), D_MODEL),
        "conv_w": nrm(ks[12], (DEPTH, CONV_WIDTH, D_FF), CONV_WIDTH),
        "conv_b": 0.01 * jax.random.normal(ks[13], (DEPTH, D_FF), f32),
        "w_down": nrm(ks[14], (DEPTH, D_FF, D_MODEL), D_FF),
        "g_final": gain(ks[15], (D_MODEL,)),
    }


def reference(x, g_mix, w_in, b_forget, g_sgu, w_spatial, b_spatial,
              w_branch_a, w_branch_b, w_out, g_ffn, w_up, conv_w, conv_b,
              w_down, g_final):
    B, S, _ = x.shape
    for l in range(DEPTH):
        h = rmsnorm(x, g_mix[l])
        q, k, v, f_logit, u, vg, gate_a, gate_b = split_in_proj(h @ w_in[l])
        log_f = jax.nn.log_sigmoid((f_logit + b_forget[l]).astype(jnp.float32))
        y_a = forgetting_attention(
            q.reshape(B, S, FOX_HEADS, FOX_HEAD_DIM),
            k.reshape(B, S, FOX_HEADS, FOX_HEAD_DIM),
            v.reshape(B, S, FOX_HEADS, FOX_HEAD_DIM),
            log_f)
        y_b = spatial_gating(jax.nn.gelu(u), jax.nn.gelu(vg),
                             g_sgu[l], w_spatial[l], b_spatial[l])
        merged = (jax.nn.sigmoid(gate_a) * (y_a @ w_branch_a[l])
                  + jax.nn.sigmoid(gate_b) * (y_b @ w_branch_b[l]))
        x = x + merged @ w_out[l]
        x = x + conv_ffn(rmsnorm(x, g_ffn[l]), w_up[l], conv_w[l], conv_b[l], w_down[l])
    return rmsnorm(x, g_final)
```

```python
import functools
import math

import jax
import jax.numpy as jnp
from jax import lax
from jax.experimental import pallas as pl
from jax.experimental.pallas import tpu as pltpu

F32 = jnp.float32
BF16 = jnp.bfloat16

HEAD_DIM = 128
SGU_GROUP_DIM = 128
SGU_CHUNK = 128
CONV_WIDTH = 3
RMS_EPS = 1e-6
LOG2E = math.log2(math.e)

LANES = 128
AUG_DIM = 2 * HEAD_DIM
HALO_ROWS = 16
VMEM_LIMIT = 56 * 1024 * 1024


def _dot(a, b):
    return jnp.dot(a, b, preferred_element_type=F32)


def _gelu(x):
    return x * (0.5 * (1.0 + jnp.tanh(0.7978845608028654 * (x + 0.044715 * (x * x * x)))))


def _sigmoid(x):
    return 1.0 / (1.0 + jnp.exp(-x))


def _rms_scale(x, gain):
    inv = lax.rsqrt(jnp.mean(x * x, axis=-1, keepdims=True) + RMS_EPS)
    return (x * inv) * gain


def _split3(x):
    hi = x.astype(BF16)
    r = x - hi.astype(F32)
    mid = r.astype(BF16)
    lo = (r - mid.astype(F32)).astype(BF16)
    return hi, mid, lo


def _params(*sem):
    return pltpu.CompilerParams(dimension_semantics=sem, vmem_limit_bytes=VMEM_LIMIT)


def _inproj_kernel(x_ref, g_ref, w_ref, wfh_ref, wfl_ref, bf_ref, o_ref, lf_ref, h_ref, *, rows):
    @pl.when(pl.program_id(1) == 0)
    def _():
        def chunk(c, carry):
            sl = pl.ds(pl.multiple_of(c * rows, rows), rows)
            h = _rms_scale(x_ref[sl, :], g_ref[...])
            hb = h.astype(BF16)
            h_ref[sl, :] = hb
            hl = (h - hb.astype(F32)).astype(BF16)
            wfh = wfh_ref[...]
            z = _dot(hb, wfh) + _dot(hl, wfh) + _dot(hb, wfl_ref[...]) + bf_ref[...]
            lf_ref[sl, :] = jnp.minimum(z, 0.0) - jnp.log(1.0 + jnp.exp(-jnp.abs(z)))
            return carry
        lax.fori_loop(0, x_ref.shape[0] // rows, chunk, 0)

    o_ref[...] = _dot(h_ref[...], w_ref[...]).astype(o_ref.dtype)


def _in_proj(x, gain, w, wf_hi, wf_lo, b_f, *, tm=1024, tn=1024, rows=128):
    m, d = x.shape
    n = w.shape[1]
    return pl.pallas_call(
        functools.partial(_inproj_kernel, rows=rows),
        grid=(m // tm, n // tn),
        in_specs=[
            pl.BlockSpec((tm, d), lambda i, j: (i, 0)),
            pl.BlockSpec((1, d), lambda i, j: (0, 0)),
            pl.BlockSpec((d, tn), lambda i, j: (0, j)),
            pl.BlockSpec((d, LANES), lambda i, j: (0, 0)),
            pl.BlockSpec((d, LANES), lambda i, j: (0, 0)),
            pl.BlockSpec((1, LANES), lambda i, j: (0, 0)),
        ],
        out_specs=[
            pl.BlockSpec((tm, tn), lambda i, j: (i, j)),
            pl.BlockSpec((tm, LANES), lambda i, j: (i, 0)),
        ],
        out_shape=[
            jax.ShapeDtypeStruct((m, n), BF16),
            jax.ShapeDtypeStruct((m, LANES), F32),
        ],
        scratch_shapes=[pltpu.VMEM((tm, d), BF16)],
        compiler_params=_params("parallel", "arbitrary"),
        name="in_proj",
    )(x, gain, w, wf_hi, wf_lo, b_f)


def _prep_kernel(lf_ref, q_ref, k_ref, qa_ref, ka_ref, carry_ref, *, heads, q_scale):
    ts = lf_ref.shape[0]

    @pl.when(pl.program_id(1) == 0)
    def _():
        carry_ref[...] = jnp.zeros_like(carry_ref)

    r = lax.broadcasted_iota(jnp.int32, (ts, ts), 0)
    c = lax.broadcasted_iota(jnp.int32, (ts, ts), 1)
    tri = jnp.where(r >= c, 1.0, 0.0).astype(BF16)
    x1, x2, x3 = _split3(lf_ref[...])
    cum = (_dot(tri, x1) + _dot(tri, x2)) + _dot(tri, x3) + carry_ref[0:1, :]
    carry_ref[0:1, :] = cum[ts - 1:ts, :]

    parts = jnp.concatenate(_split3(cum * LOG2E), axis=-1)
    row = lax.broadcasted_iota(jnp.int32, (3 * LANES, LANES), 0)
    lane = lax.broadcasted_iota(jnp.int32, (3 * LANES, LANES), 1)
    part, src = row // LANES, row % LANES
    lane1 = lax.broadcasted_iota(jnp.int32, (ts, LANES), 1)
    for h in range(heads):
        sel_q = jnp.where((src == h) & (lane == part), 1.0, 0.0).astype(BF16)
        sel_k = jnp.where((src == h) & (lane == part + 3), -1.0, 0.0).astype(BF16)
        ext_q = jnp.where((lane1 >= 3) & (lane1 < 6), 1.0, _dot(parts, sel_q))
        ext_k = jnp.where(lane1 < 3, 1.0, _dot(parts, sel_k))
        qh = q_ref[:, h * HEAD_DIM:(h + 1) * HEAD_DIM].astype(F32) * q_scale
        qa_ref[:, h * AUG_DIM:h * AUG_DIM + HEAD_DIM] = qh.astype(BF16)
        qa_ref[:, h * AUG_DIM + HEAD_DIM:(h + 1) * AUG_DIM] = ext_q.astype(BF16)
        ka_ref[:, h * AUG_DIM:h * AUG_DIM + HEAD_DIM] = k_ref[:, h * HEAD_DIM:(h + 1) * HEAD_DIM]
        ka_ref[:, h * AUG_DIM + HEAD_DIM:(h + 1) * AUG_DIM] = ext_k.astype(BF16)


def _fox_prep(logf, proj, *, batch, heads, ts=256):
    m = logf.shape[0]
    seq = m // batch
    width = heads * HEAD_DIM
    nblk = seq // ts
    return pl.pallas_call(
        functools.partial(_prep_kernel, heads=heads, q_scale=HEAD_DIM ** -0.5 * LOG2E),
        grid=(batch, nblk),
        in_specs=[
            pl.BlockSpec((ts, LANES), lambda b, s: (b * nblk + s, 0)),
            pl.BlockSpec((ts, width), lambda b, s: (b * nblk + s, 0)),
            pl.BlockSpec((ts, width), lambda b, s: (b * nblk + s, 1)),
        ],
        out_specs=[
            pl.BlockSpec((ts, heads * AUG_DIM), lambda b, s: (b * nblk + s, 0)),
            pl.BlockSpec((ts, heads * AUG_DIM), lambda b, s: (b * nblk + s, 0)),
        ],
        out_shape=[
            jax.ShapeDtypeStruct((m, heads * AUG_DIM), BF16),
            jax.ShapeDtypeStruct((m, heads * AUG_DIM), BF16),
        ],
        scratch_shapes=[pltpu.VMEM((8, LANES), F32)],
        compiler_params=_params("parallel", "arbitrary"),
        name="fox_prep",
    )(logf, proj, proj)


def _attn_kernel(qa_ref, ka_ref, v_ref, o_ref, *, hg, tq, tk):
    i = pl.program_id(2)
    lane = lax.broadcasted_iota(jnp.int32, (tk, HEAD_DIM), 1)
    ones_col = jnp.where(lane == 0, 1.0, 0.0).astype(BF16)

    def block(h, j0, carry, masked):
        m_prev, acc = carry
        q = qa_ref[:, h * AUG_DIM:(h + 1) * AUG_DIM]
        k = ka_ref[pl.ds(j0, tk), h * AUG_DIM:(h + 1) * AUG_DIM]
        v = v_ref[pl.ds(j0, tk), h * HEAD_DIM:(h + 1) * HEAD_DIM]
        s = lax.dot_general(q, k, (((1,), (1,)), ((), ())), preferred_element_type=F32)
        if masked:
            qpos = i * tq + lax.broadcasted_iota(jnp.int32, (tq, tk), 0)
            kpos = j0 + lax.broadcasted_iota(jnp.int32, (tq, tk), 1)
            s = jnp.where(kpos <= qpos, s, -1e30)
        m_new = jnp.maximum(m_prev, jnp.max(s, axis=-1, keepdims=True))
        p = jnp.exp2(s - m_new).astype(BF16)
        pv = _dot(p, jnp.concatenate([v, ones_col], axis=-1))
        return m_new, acc * jnp.exp2(m_prev - m_new) + pv

    def init():
        return (jnp.full((tq, 1), -1e30, F32), jnp.zeros((tq, AUG_DIM), F32))

    def body(j, carries):
        j0 = pl.multiple_of(j * tk, tk)
        return tuple(block(h, j0, carries[h], False) for h in range(hg))

    carries = lax.fori_loop(0, i * (tq // tk), body, tuple(init() for _ in range(hg)))
    for d in range(tq // tk):
        j0 = pl.multiple_of(i * tq + d * tk, tk)
        carries = tuple(block(h, j0, carries[h], True) for h in range(hg))
    for h in range(hg):
        acc = carries[h][1]
        out = acc[:, :HEAD_DIM] / acc[:, HEAD_DIM:HEAD_DIM + 1]
        o_ref[:, h * HEAD_DIM:(h + 1) * HEAD_DIM] = out.astype(o_ref.dtype)


def _fox_attn(q_aug, k_aug, proj, *, batch, heads, hg=2, tq=512, tk=512):
    m = q_aug.shape[0]
    seq = m // batch
    nq = seq // tq
    v_col0 = 2 * heads // hg
    return pl.pallas_call(
        functools.partial(_attn_kernel, hg=hg, tq=tq, tk=tk),
        grid=(batch, heads // hg, nq),
        in_specs=[
            pl.BlockSpec((tq, hg * AUG_DIM), lambda b, g, i: (b * nq + i, g)),
            pl.BlockSpec((seq, hg * AUG_DIM), lambda b, g, i: (b, g)),
            pl.BlockSpec((seq, hg * HEAD_DIM), lambda b, g, i: (b, v_col0 + g)),
        ],
        out_specs=pl.BlockSpec((tq, hg * HEAD_DIM), lambda b, g, i: (b * nq + i, g)),
        out_shape=jax.ShapeDtypeStruct((m, heads * HEAD_DIM), BF16),
        compiler_params=_params("parallel", "parallel", "arbitrary"),
        name="fox_attn",
    )(q_aug, k_aug, proj)


def _sgu_kernel(u_ref, v_ref, g_ref, w_ref, bt_ref, o_ref, *, groups):
    ts = u_ref.shape[0]
    vn = _rms_scale(_gelu(v_ref[...].astype(F32)), g_ref[...]).astype(BF16)
    r = lax.broadcasted_iota(jnp.int32, (SGU_CHUNK, SGU_CHUNK), 0)
    c = lax.broadcasted_iota(jnp.int32, (SGU_CHUNK, SGU_CHUNK), 1)
    for g in range(groups):
        w = jnp.where(r >= c, w_ref[g], 0.0).astype(BF16)
        bias = bt_ref[:, g:g + 1]
        cols = slice(g * SGU_GROUP_DIM, (g + 1) * SGU_GROUP_DIM)
        for ch in range(ts // SGU_CHUNK):
            rows = slice(ch * SGU_CHUNK, (ch + 1) * SGU_CHUNK)
            mixed = _dot(w, vn[rows, cols]) + bias
            o_ref[rows, cols] = (_gelu(u_ref[rows, cols].astype(F32)) * mixed).astype(o_ref.dtype)


def _sgu(proj, gain, w_s, b_s_t, *, u_col, ts=256):
    m = proj.shape[0]
    groups = w_s.shape[0]
    width = groups * SGU_GROUP_DIM
    return pl.pallas_call(
        functools.partial(_sgu_kernel, groups=groups),
        grid=(m // ts,),
        in_specs=[
            pl.BlockSpec((ts, width), lambda i: (i, u_col)),
            pl.BlockSpec((ts, width), lambda i: (i, u_col + 1)),
            pl.BlockSpec((1, width), lambda i: (0, 0)),
            pl.BlockSpec((groups, SGU_CHUNK, SGU_CHUNK), lambda i: (0, 0, 0)),
            pl.BlockSpec((SGU_CHUNK, groups), lambda i: (0, 0)),
        ],
        out_specs=pl.BlockSpec((ts, width), lambda i: (i, 0)),
        out_shape=jax.ShapeDtypeStruct((m, width), BF16),
        compiler_params=_params("parallel"),
        name="sgu",
    )(proj, proj, gain, w_s, b_s_t)


def _merge_kernel(ya_ref, yb_ref, wa_ref, wb_ref, ga_ref, gb_ref, o_ref):
    a = _sigmoid(ga_ref[...].astype(F32)) * _dot(ya_ref[...], wa_ref[...])
    b = _sigmoid(gb_ref[...].astype(F32)) * _dot(yb_ref[...], wb_ref[...])
    o_ref[...] = (a + b).astype(o_ref.dtype)


def _merge(y_a, y_b, w_a, w_b, proj, *, gate_col, tm=1024, tn=512):
    m, ka = y_a.shape
    kb = y_b.shape[1]
    n = w_a.shape[1]
    nb = n // tn
    ga0, gb0 = gate_col // tn, gate_col // tn + nb
    return pl.pallas_call(
        _merge_kernel,
        grid=(m // tm, nb),
        in_specs=[
            pl.BlockSpec((tm, ka), lambda i, j: (i, 0)),
            pl.BlockSpec((tm, kb), lambda i, j: (i, 0)),
            pl.BlockSpec((ka, tn), lambda i, j: (0, j)),
            pl.BlockSpec((kb, tn), lambda i, j: (0, j)),
            pl.BlockSpec((tm, tn), lambda i, j: (i, ga0 + j)),
            pl.BlockSpec((tm, tn), lambda i, j: (i, gb0 + j)),
        ],
        out_specs=pl.BlockSpec((tm, tn), lambda i, j: (i, j)),
        out_shape=jax.ShapeDtypeStruct((m, n), BF16),
        compiler_params=_params("parallel", "parallel"),
        name="merge",
    )(y_a, y_b, w_a, w_b, proj, proj)


def _resid_kernel(a_ref, w_ref, x_ref, o_ref):
    o_ref[...] = x_ref[...] + _dot(a_ref[...], w_ref[...])


def _resid_matmul(a, w, x, *, tm=1024, tn=512, name):
    m, k = a.shape
    n = w.shape[1]
    return pl.pallas_call(
        _resid_kernel,
        grid=(m // tm, n // tn),
        in_specs=[
            pl.BlockSpec((tm, k), lambda i, j: (i, 0)),
            pl.BlockSpec((k, tn), lambda i, j: (0, j)),
            pl.BlockSpec((tm, tn), lambda i, j: (i, j)),
        ],
        out_specs=pl.BlockSpec((tm, tn), lambda i, j: (i, j)),
        out_shape=jax.ShapeDtypeStruct((m, n), F32),
        compiler_params=_params("parallel", "parallel"),
        name=name,
    )(a, w, x)


def _ffn_up_kernel(x_ref, xh_ref, g_ref, wa_ref, wb_ref, cw_ref, cb_ref, o_ref, h_ref, hh_ref, a_ref,
                   *, rows, tiles_per_seq):
    tm = x_ref.shape[0]

    @pl.when(pl.program_id(1) == 0)
    def _():
        def chunk(c, carry):
            sl = pl.ds(pl.multiple_of(c * rows, rows), rows)
            h_ref[sl, :] = _rms_scale(x_ref[sl, :], g_ref[...]).astype(BF16)
            return carry
        lax.fori_loop(0, tm // rows, chunk, 0)
        keep = jnp.where(pl.program_id(0) % tiles_per_seq == 0, 0.0, 1.0)
        hh_ref[...] = (_rms_scale(xh_ref[...], g_ref[...]) * keep).astype(BF16)

    wa = wa_ref[...]
    a_ref[0:HALO_ROWS, :] = _dot(hh_ref[...], wa)
    a = _dot(h_ref[...], wa)
    a_ref[HALO_ROWS:, :] = a
    conv = (cb_ref[...] + cw_ref[0:1, :] * a_ref[pl.ds(HALO_ROWS - 2, tm), :]
            + cw_ref[1:2, :] * a_ref[pl.ds(HALO_ROWS - 1, tm), :] + cw_ref[2:3, :] * a)
    o_ref[...] = (_gelu(conv) * _dot(h_ref[...], wb_ref[...])).astype(o_ref.dtype)


def _ffn_up(x, gain, w_up, conv_w, conv_b, *, seq, tm=1024, tn=512, rows=128):
    m, d = x.shape
    dff = w_up.shape[1] // 2
    nb = dff // tn
    halo_blocks = tm // HALO_ROWS
    return pl.pallas_call(
        functools.partial(_ffn_up_kernel, rows=rows, tiles_per_seq=seq // tm),
        grid=(m // tm, nb),
        in_specs=[
            pl.BlockSpec((tm, d), lambda i, j: (i, 0)),
            pl.BlockSpec((HALO_ROWS, d), lambda i, j: (jnp.maximum(i * halo_blocks - 1, 0), 0)),
            pl.BlockSpec((1, d), lambda i, j: (0, 0)),
            pl.BlockSpec((d, tn), lambda i, j: (0, j)),
            pl.BlockSpec((d, tn), lambda i, j: (0, nb + j)),
            pl.BlockSpec((CONV_WIDTH, tn), lambda i, j: (0, j)),
            pl.BlockSpec((1, tn), lambda i, j: (0, j)),
        ],
        out_specs=pl.BlockSpec((tm, tn), lambda i, j: (i, j)),
        out_shape=jax.ShapeDtypeStruct((m, dff), BF16),
        scratch_shapes=[
            pltpu.VMEM((tm, d), BF16),
            pltpu.VMEM((HALO_ROWS, d), BF16),
            pltpu.VMEM((HALO_ROWS + tm, tn), F32),
        ],
        compiler_params=_params("parallel", "arbitrary"),
        name="ffn_up",
    )(x, x, gain, w_up, w_up, conv_w, conv_b)


def _norm_kernel(x_ref, g_ref, o_ref):
    o_ref[...] = _rms_scale(x_ref[...], g_ref[...])


def _final_norm(x, gain, *, tm=256):
    m, d = x.shape
    return pl.pallas_call(
        _norm_kernel,
        grid=(m // tm,),
        in_specs=[pl.BlockSpec((tm, d), lambda i: (i, 0)), pl.BlockSpec((1, d), lambda i: (0, 0))],
        out_specs=pl.BlockSpec((tm, d), lambda i: (i, 0)),
        out_shape=jax.ShapeDtypeStruct((m, d), F32),
        compiler_params=_params("parallel"),
        name="final_norm",
    )(x, gain)


def kernel(x, g_mix, w_in, b_forget, g_sgu, w_spatial, b_spatial, w_branch_a, w_branch_b, w_out, g_ffn,
           w_up, conv_w, conv_b, w_down, g_final):
    batch, seq, d = x.shape
    depth = w_in.shape[0]
    heads = b_forget.shape[1]
    fox = heads * HEAD_DIM
    sgu_w = g_sgu.shape[1]
    f0 = 3 * fox
    u0 = 3 * fox
    gate0 = u0 + 2 * sgu_w
    assert fox == sgu_w, "column-block indexing of proj assumes equal branch widths"

    xs = x.reshape(batch * seq, d)
    for l in range(depth):
        w_l = w_in[l]
        w_main = jnp.concatenate([w_l[:, :f0], w_l[:, f0 + heads:]], axis=1).astype(BF16)
        w_f = jnp.pad(w_l[:, f0:f0 + heads], ((0, 0), (0, LANES - heads)))
        wf_hi = w_f.astype(BF16)
        wf_lo = (w_f - wf_hi.astype(F32)).astype(BF16)
        b_f = jnp.pad(b_forget[l], (0, LANES - heads)).reshape(1, LANES)

        proj, logf = _in_proj(xs, g_mix[l].reshape(1, d), w_main, wf_hi, wf_lo, b_f)
        q_aug, k_aug = _fox_prep(logf, proj, batch=batch, heads=heads)
        y_a = _fox_attn(q_aug, k_aug, proj, batch=batch, heads=heads)
        y_b = _sgu(proj, g_sgu[l].reshape(1, sgu_w), w_spatial[l], b_spatial[l].T, u_col=u0 // sgu_w)
        merged = _merge(y_a, y_b, w_branch_a[l].astype(BF16), w_branch_b[l].astype(BF16), proj,
                        gate_col=gate0)
        xs = _resid_matmul(merged, w_out[l].astype(BF16), xs, name="out_proj")
        g = _ffn_up(xs, g_ffn[l].reshape(1, d), w_up[l].astype(BF16), conv_w[l],
                    conv_b[l].reshape(1, -1), seq=seq)
        xs = _resid_matmul(g, w_down[l].astype(BF16), xs, name="ffn_down")
    return _final_norm(xs, g_final.reshape(1, d)).reshape(batch, seq, d)
```

```python
import functools
import math

import jax
import jax.numpy as jnp
from jax import lax
from jax.experimental import pallas as pl
from jax.experimental.pallas import tpu as pltpu

F32 = jnp.float32
BF16 = jnp.bfloat16

HEAD_DIM = 128
SGU_GROUP_DIM = 128
SGU_CHUNK = 128
CONV_WIDTH = 3
RMS_EPS = 1e-6
LOG2E = math.log2(math.e)

LANES = 128
AUG_DIM = 2 * HEAD_DIM
HALO_ROWS = 8
VMEM_LIMIT = 56 * 1024 * 1024


def _dot(a, b):
    return jnp.dot(a, b, preferred_element_type=F32)


def _gelu(x):
    return x * (0.5 * (1.0 + jnp.tanh(0.7978845608028654 * (x + 0.044715 * (x * x * x)))))


def _sigmoid(x):
    return 1.0 / (1.0 + jnp.exp(-x))


def _rms_scale(x, gain):
    inv = lax.rsqrt(jnp.mean(x * x, axis=-1, keepdims=True) + RMS_EPS)
    return (x * inv) * gain


def _split3(x):
    hi = x.astype(BF16)
    r = x - hi.astype(F32)
    mid = r.astype(BF16)
    lo = (r - mid.astype(F32)).astype(BF16)
    return hi, mid, lo


def _params(*sem):
    return pltpu.CompilerParams(dimension_semantics=sem, vmem_limit_bytes=VMEM_LIMIT)


def _inproj_kernel(x_ref, g_ref, w_ref, wfh_ref, wfl_ref, bf_ref, o_ref, lf_ref, h_ref, *, rows):
    @pl.when(pl.program_id(1) == 0)
    def _():
        def chunk(c, carry):
            sl = pl.ds(pl.multiple_of(c * rows, rows), rows)
            h = _rms_scale(x_ref[sl, :], g_ref[...])
            hb = h.astype(BF16)
            h_ref[sl, :] = hb
            hl = (h - hb.astype(F32)).astype(BF16)
            wfh = wfh_ref[...]
            z = _dot(hb, wfh) + _dot(hl, wfh) + _dot(hb, wfl_ref[...]) + bf_ref[...]
            lf_ref[sl, :] = jnp.minimum(z, 0.0) - jnp.log(1.0 + jnp.exp(-jnp.abs(z)))
            return carry
        lax.fori_loop(0, x_ref.shape[0] // rows, chunk, 0)

    o_ref[...] = _dot(h_ref[...], w_ref[...]).astype(o_ref.dtype)


def _in_proj(x, gain, w, wf_hi, wf_lo, b_f, layer, *, tm=1024, tn=1536, rows=128):
    m, d = x.shape
    n = w.shape[2]
    return pl.pallas_call(
        functools.partial(_inproj_kernel, rows=rows),
        grid=(m // tm, n // tn),
        in_specs=[
            pl.BlockSpec((tm, d), lambda i, j: (i, 0)),
            pl.BlockSpec((None, 1, d), lambda i, j: (layer, 0, 0)),
            pl.BlockSpec((None, d, tn), lambda i, j: (layer, 0, j)),
            pl.BlockSpec((None, d, LANES), lambda i, j: (layer, 0, 0)),
            pl.BlockSpec((None, d, LANES), lambda i, j: (layer, 0, 0)),
            pl.BlockSpec((None, 1, LANES), lambda i, j: (layer, 0, 0)),
        ],
        out_specs=[
            pl.BlockSpec((tm, tn), lambda i, j: (i, j)),
            pl.BlockSpec((tm, LANES), lambda i, j: (i, 0)),
        ],
        out_shape=[
            jax.ShapeDtypeStruct((m, n), BF16),
            jax.ShapeDtypeStruct((m, LANES), F32),
        ],
        scratch_shapes=[pltpu.VMEM((tm, d), BF16)],
        compiler_params=_params("parallel", "arbitrary"),
        name="in_proj",
    )(x, gain, w, wf_hi, wf_lo, b_f)


def _prep_kernel(lf_ref, q_ref, k_ref, qa_ref, ka_ref, carry_ref, *, heads, q_scale):
    ts = lf_ref.shape[0]

    @pl.when(pl.program_id(1) == 0)
    def _():
        carry_ref[...] = jnp.zeros_like(carry_ref)

    r = lax.broadcasted_iota(jnp.int32, (ts, ts), 0)
    c = lax.broadcasted_iota(jnp.int32, (ts, ts), 1)
    tri = jnp.where(r >= c, 1.0, 0.0).astype(BF16)
    x1, x2, x3 = _split3(lf_ref[...])
    cum = (_dot(tri, x1) + _dot(tri, x2)) + _dot(tri, x3) + carry_ref[0:1, :]
    carry_ref[0:1, :] = cum[ts - 1:ts, :]

    parts = jnp.concatenate(_split3(cum * LOG2E), axis=-1)
    row = lax.broadcasted_iota(jnp.int32, (3 * LANES, LANES), 0)
    lane = lax.broadcasted_iota(jnp.int32, (3 * LANES, LANES), 1)
    part, src = row // LANES, row % LANES
    lane1 = lax.broadcasted_iota(jnp.int32, (ts, LANES), 1)
    for h in range(heads):
        sel_q = jnp.where((src == h) & (lane == part), 1.0, 0.0).astype(BF16)
        sel_k = jnp.where((src == h) & (lane == part + 3), -1.0, 0.0).astype(BF16)
        ext_q = jnp.where((lane1 >= 3) & (lane1 < 6), 1.0, _dot(parts, sel_q))
        ext_k = jnp.where(lane1 < 3, 1.0, _dot(parts, sel_k))
        qh = q_ref[:, h * HEAD_DIM:(h + 1) * HEAD_DIM].astype(F32) * q_scale
        qa_ref[:, h * AUG_DIM:h * AUG_DIM + HEAD_DIM] = qh.astype(BF16)
        qa_ref[:, h * AUG_DIM + HEAD_DIM:(h + 1) * AUG_DIM] = ext_q.astype(BF16)
        ka_ref[:, h * AUG_DIM:h * AUG_DIM + HEAD_DIM] = k_ref[:, h * HEAD_DIM:(h + 1) * HEAD_DIM]
        ka_ref[:, h * AUG_DIM + HEAD_DIM:(h + 1) * AUG_DIM] = ext_k.astype(BF16)


def _fox_prep(logf, proj, *, batch, heads, ts=256):
    m = logf.shape[0]
    seq = m // batch
    width = heads * HEAD_DIM
    nblk = seq // ts
    return pl.pallas_call(
        functools.partial(_prep_kernel, heads=heads, q_scale=HEAD_DIM ** -0.5 * LOG2E),
        grid=(batch, nblk),
        in_specs=[
            pl.BlockSpec((ts, LANES), lambda b, s: (b * nblk + s, 0)),
            pl.BlockSpec((ts, width), lambda b, s: (b * nblk + s, 0)),
            pl.BlockSpec((ts, width), lambda b, s: (b * nblk + s, 1)),
        ],
        out_specs=[
            pl.BlockSpec((ts, heads * AUG_DIM), lambda b, s: (b * nblk + s, 0)),
            pl.BlockSpec((ts, heads * AUG_DIM), lambda b, s: (b * nblk + s, 0)),
        ],
        out_shape=[
            jax.ShapeDtypeStruct((m, heads * AUG_DIM), BF16),
            jax.ShapeDtypeStruct((m, heads * AUG_DIM), BF16),
        ],
        scratch_shapes=[pltpu.VMEM((8, LANES), F32)],
        compiler_params=_params("parallel", "arbitrary"),
        name="fox_prep",
    )(logf, proj, proj)


def _attn_kernel(qa_ref, ka_ref, v_ref, o_ref, acc_ref, m_ref, s_ref, *, hg, tq, tk):
    i = pl.program_id(2)
    lane = lax.broadcasted_iota(jnp.int32, (tk, HEAD_DIM), 1)
    ones_col = jnp.where(lane == 0, 1.0, 0.0).astype(BF16)

    def scores(h, j0, r0):
        q = qa_ref[r0:, h * AUG_DIM:(h + 1) * AUG_DIM]
        k = ka_ref[pl.ds(j0, tk), h * AUG_DIM:(h + 1) * AUG_DIM]
        return lax.dot_general(q, k, (((1,), (1,)), ((), ())), preferred_element_type=F32)

    def fold(h, s, j0, r0, masked):
        v = v_ref[pl.ds(j0, tk), h * HEAD_DIM:(h + 1) * HEAD_DIM]
        if masked:
            qpos = i * tq + r0 + lax.broadcasted_iota(jnp.int32, s.shape, 0)
            kpos = j0 + lax.broadcasted_iota(jnp.int32, s.shape, 1)
            s = jnp.where(kpos <= qpos, s, -1e30)
        m_prev = m_ref[h, r0:, :]
        m_new = jnp.maximum(m_prev, jnp.max(s, axis=-1, keepdims=True))
        m_ref[h, r0:, :] = m_new
        p = jnp.exp2(s - jnp.tile(m_new, (1, tk // LANES))).astype(BF16)
        pv = _dot(p, jnp.concatenate([v, ones_col], axis=-1))
        alpha = jnp.tile(jnp.exp2(m_prev - m_new), (1, AUG_DIM // LANES))
        acc_ref[h, r0:, :] = acc_ref[h, r0:, :] * alpha + pv

    m_ref[...] = jnp.full(m_ref.shape, -1e30, F32)
    acc_ref[...] = jnp.zeros(acc_ref.shape, F32)

    for h in range(hg):
        s_ref[h] = scores(h, 0, 0)

    def body(j, carry):
        j0 = pl.multiple_of(j * tk, tk)
        for h in range(hg):
            s = s_ref[h]
            s_ref[h] = scores(h, j0 + tk, 0)
            fold(h, s, j0, 0, False)
        return carry

    lax.fori_loop(0, i * (tq // tk), body, 0)
    nd = tq // tk
    for d in range(nd):
        j0 = pl.multiple_of(i * tq + d * tk, tk)
        r0 = d * tk
        for h in range(hg):
            s = s_ref[h, r0:, :]
            if d + 1 < nd:
                s_ref[h, r0 + tk:, :] = scores(h, j0 + tk, r0 + tk)
            fold(h, s, j0, r0, True)
    for h in range(hg):
        acc = acc_ref[h]
        out = acc[:, :HEAD_DIM] / acc[:, HEAD_DIM:HEAD_DIM + 1]
        o_ref[:, h * HEAD_DIM:(h + 1) * HEAD_DIM] = out.astype(o_ref.dtype)


def _fox_attn(q_aug, k_aug, proj, *, batch, heads, hg=2, tq=1024, tk=512):
    m = q_aug.shape[0]
    seq = m // batch
    nq = seq // tq
    v_col0 = 2 * heads // hg
    return pl.pallas_call(
        functools.partial(_attn_kernel, hg=hg, tq=tq, tk=tk),
        grid=(batch, heads // hg, nq),
        in_specs=[
            pl.BlockSpec((tq, hg * AUG_DIM), lambda b, g, i: (b * nq + i, g)),
            pl.BlockSpec((seq, hg * AUG_DIM), lambda b, g, i: (b, g)),
            pl.BlockSpec((seq, hg * HEAD_DIM), lambda b, g, i: (b, v_col0 + g)),
        ],
        out_specs=pl.BlockSpec((tq, hg * HEAD_DIM), lambda b, g, i: (b * nq + i, g)),
        out_shape=jax.ShapeDtypeStruct((m, heads * HEAD_DIM), BF16),
        scratch_shapes=[pltpu.VMEM((hg, tq, AUG_DIM), F32), pltpu.VMEM((hg, tq, LANES), F32),
                        pltpu.VMEM((hg, tq, tk), F32)],
        compiler_params=_params("parallel", "parallel", "arbitrary"),
        name="fox_attn",
    )(q_aug, k_aug, proj)


def _sgu_kernel(u_ref, v_ref, g_ref, w_ref, bt_ref, o_ref, *, groups):
    ts = u_ref.shape[0]
    vn = _rms_scale(_gelu(v_ref[...].astype(F32)), g_ref[...]).astype(BF16)
    r = lax.broadcasted_iota(jnp.int32, (SGU_CHUNK, SGU_CHUNK), 0)
    c = lax.broadcasted_iota(jnp.int32, (SGU_CHUNK, SGU_CHUNK), 1)
    for g in range(groups):
        w = jnp.where(r >= c, w_ref[g], 0.0).astype(BF16)
        bias = bt_ref[:, g:g + 1]
        cols = slice(g * SGU_GROUP_DIM, (g + 1) * SGU_GROUP_DIM)
        for ch in range(ts // SGU_CHUNK):
            rows = slice(ch * SGU_CHUNK, (ch + 1) * SGU_CHUNK)
            mixed = _dot(w, vn[rows, cols]) + bias
            o_ref[rows, cols] = (_gelu(u_ref[rows, cols].astype(F32)) * mixed).astype(o_ref.dtype)


def _sgu(proj, gain, w_s, b_s_t, layer, *, u_col, ts=256):
    m = proj.shape[0]
    groups = w_s.shape[1]
    width = groups * SGU_GROUP_DIM
    return pl.pallas_call(
        functools.partial(_sgu_kernel, groups=groups),
        grid=(m // ts,),
        in_specs=[
            pl.BlockSpec((ts, width), lambda i: (i, u_col)),
            pl.BlockSpec((ts, width), lambda i: (i, u_col + 1)),
            pl.BlockSpec((None, 1, width), lambda i: (layer, 0, 0)),
            pl.BlockSpec((None, groups, SGU_CHUNK, SGU_CHUNK), lambda i: (layer, 0, 0, 0)),
            pl.BlockSpec((None, SGU_CHUNK, groups), lambda i: (layer, 0, 0)),
        ],
        out_specs=pl.BlockSpec((ts, width), lambda i: (i, 0)),
        out_shape=jax.ShapeDtypeStruct((m, width), BF16),
        compiler_params=_params("parallel"),
        name="sgu",
    )(proj, proj, gain, w_s, b_s_t)


def _merge_kernel(ya_ref, yb_ref, wa_ref, wb_ref, ga_ref, gb_ref, o_ref):
    a = _sigmoid(ga_ref[...].astype(F32)) * _dot(ya_ref[...], wa_ref[...])
    b = _sigmoid(gb_ref[...].astype(F32)) * _dot(yb_ref[...], wb_ref[...])
    o_ref[...] = (a + b).astype(o_ref.dtype)


def _merge(y_a, y_b, w_a, w_b, proj, layer, *, gate_col, tm=1024, tn=1024):
    m, ka = y_a.shape
    kb = y_b.shape[1]
    n = w_a.shape[2]
    nb = n // tn
    ga0, gb0 = gate_col // tn, gate_col // tn + nb
    return pl.pallas_call(
        _merge_kernel,
        grid=(m // tm, nb),
        in_specs=[
            pl.BlockSpec((tm, ka), lambda i, j: (i, 0)),
            pl.BlockSpec((tm, kb), lambda i, j: (i, 0)),
            pl.BlockSpec((None, ka, tn), lambda i, j: (layer, 0, j)),
            pl.BlockSpec((None, kb, tn), lambda i, j: (layer, 0, j)),
            pl.BlockSpec((tm, tn), lambda i, j: (i, ga0 + j)),
            pl.BlockSpec((tm, tn), lambda i, j: (i, gb0 + j)),
        ],
        out_specs=pl.BlockSpec((tm, tn), lambda i, j: (i, j)),
        out_shape=jax.ShapeDtypeStruct((m, n), BF16),
        compiler_params=_params("parallel", "parallel"),
        name="merge",
    )(y_a, y_b, w_a, w_b, proj, proj)


def _outproj_kernel(a_ref, w_ref, x_ref, g_ref, xo_ref, h_ref):
    x_new = x_ref[...] + _dot(a_ref[...], w_ref[...])
    xo_ref[...] = x_new
    h_ref[...] = _rms_scale(x_new, g_ref[...]).astype(h_ref.dtype)


def _out_proj(a, w, x, gain, layer, *, tm=512):
    m, k = a.shape
    n = w.shape[2]
    return pl.pallas_call(
        _outproj_kernel,
        grid=(m // tm,),
        in_specs=[
            pl.BlockSpec((tm, k), lambda i: (i, 0)),
            pl.BlockSpec((None, k, n), lambda i: (layer, 0, 0)),
            pl.BlockSpec((tm, n), lambda i: (i, 0)),
            pl.BlockSpec((None, 1, n), lambda i: (layer, 0, 0)),
        ],
        out_specs=[pl.BlockSpec((tm, n), lambda i: (i, 0)), pl.BlockSpec((tm, n), lambda i: (i, 0))],
        out_shape=[jax.ShapeDtypeStruct((m, n), F32), jax.ShapeDtypeStruct((m, n), BF16)],
        compiler_params=_params("parallel"),
        name="out_proj",
    )(a, w, x, gain)


def _ffn_up_kernel(h_ref, wa_ref, wb_ref, cw_ref, cb_ref, o_ref, a_ref, tail_ref, *, tiles_per_seq, rows):
    tm = h_ref.shape[0]
    j = pl.program_id(1)
    seq_start = pl.program_id(0) % tiles_per_seq == 0

    @pl.when(seq_start)
    def _():
        a_ref[0:HALO_ROWS, :] = jnp.zeros((HALO_ROWS, a_ref.shape[1]), F32)

    @pl.when(jnp.logical_not(seq_start))
    def _():
        a_ref[0:HALO_ROWS, :] = tail_ref[j]

    for r0 in range(0, tm, rows):
        h = h_ref[r0:r0 + rows, :]
        a = _dot(h, wa_ref[...])
        a_ref[HALO_ROWS + r0:HALO_ROWS + r0 + rows, :] = a
        conv = (cb_ref[...] + cw_ref[0:1, :] * a_ref[pl.ds(HALO_ROWS - 2 + r0, rows), :]
                + cw_ref[1:2, :] * a_ref[pl.ds(HALO_ROWS - 1 + r0, rows), :] + cw_ref[2:3, :] * a)
        o_ref[r0:r0 + rows, :] = (_gelu(conv) * _dot(h, wb_ref[...])).astype(o_ref.dtype)
    tail_ref[j] = a_ref[tm:, :]


def _ffn_up(h, w_up, conv_w, conv_b, layer, *, seq, tm=1024, tn=512, rows=1024):
    m, d = h.shape
    dff = w_up.shape[2] // 2
    nb = dff // tn
    return pl.pallas_call(
        functools.partial(_ffn_up_kernel, tiles_per_seq=seq // tm, rows=rows),
        grid=(m // tm, nb),
        in_specs=[
            pl.BlockSpec((tm, d), lambda i, j: (i, 0)),
            pl.BlockSpec((None, d, tn), lambda i, j: (layer, 0, j)),
            pl.BlockSpec((None, d, tn), lambda i, j: (layer, 0, nb + j)),
            pl.BlockSpec((None, CONV_WIDTH, tn), lambda i, j: (layer, 0, j)),
            pl.BlockSpec((None, 1, tn), lambda i, j: (layer, 0, j)),
        ],
        out_specs=pl.BlockSpec((tm, tn), lambda i, j: (i, j)),
        out_shape=jax.ShapeDtypeStruct((m, dff), BF16),
        scratch_shapes=[pltpu.VMEM((HALO_ROWS + tm, tn), F32), pltpu.VMEM((nb, HALO_ROWS, tn), F32)],
        compiler_params=_params("arbitrary", "arbitrary"),
        name="ffn_up",
    )(h, w_up, w_up, conv_w, conv_b)


def _ffn_down_kernel(a_ref, w_ref, x_ref, o_ref):
    o_ref[...] = x_ref[...] + _dot(a_ref[...], w_ref[...])


def _ffn_down(a, w, x, layer, *, tm=1024, tn=512):
    m, k = a.shape
    n = w.shape[2]
    return pl.pallas_call(
        _ffn_down_kernel,
        grid=(m // tm, n // tn),
        in_specs=[
            pl.BlockSpec((tm, k), lambda i, j: (i, 0)),
            pl.BlockSpec((None, k, tn), lambda i, j: (layer, 0, j)),
            pl.BlockSpec((tm, tn), lambda i, j: (i, j)),
        ],
        out_specs=pl.BlockSpec((tm, tn), lambda i, j: (i, j)),
        out_shape=jax.ShapeDtypeStruct((m, n), F32),
        compiler_params=_params("parallel", "parallel"),
        name="ffn_down",
    )(a, w, x)


def _norm_kernel(x_ref, g_ref, o_ref):
    o_ref[...] = _rms_scale(x_ref[...], g_ref[...])


def _final_norm(x, gain, *, tm=256):
    m, d = x.shape
    return pl.pallas_call(
        _norm_kernel,
        grid=(m // tm,),
        in_specs=[pl.BlockSpec((tm, d), lambda i: (i, 0)), pl.BlockSpec((1, d), lambda i: (0, 0))],
        out_specs=pl.BlockSpec((tm, d), lambda i: (i, 0)),
        out_shape=jax.ShapeDtypeStruct((m, d), F32),
        compiler_params=_params("parallel"),
        name="final_norm",
    )(x, gain)


def kernel(x, g_mix, w_in, b_forget, g_sgu, w_spatial, b_spatial, w_branch_a, w_branch_b, w_out, g_ffn,
           w_up, conv_w, conv_b, w_down, g_final):
    batch, seq, d = x.shape
    depth = w_in.shape[0]
    heads = b_forget.shape[1]
    fox = heads * HEAD_DIM
    sgu_w = g_sgu.shape[1]
    f0 = 3 * fox
    u0 = 3 * fox
    gate0 = u0 + 2 * sgu_w
    assert fox == sgu_w, "column-block indexing of proj assumes equal branch widths"

    w_main = jnp.concatenate([w_in[:, :, :f0], w_in[:, :, f0 + heads:]], axis=2).astype(BF16)
    w_f = jnp.pad(w_in[:, :, f0:f0 + heads], ((0, 0), (0, 0), (0, LANES - heads)))
    wf_hi = w_f.astype(BF16)
    wf_lo = (w_f - wf_hi.astype(F32)).astype(BF16)
    b_f = jnp.pad(b_forget, ((0, 0), (0, LANES - heads))).reshape(depth, 1, LANES)
    w_a_b, w_b_b, w_out_b, w_up_b, w_down_b = (
        w.astype(BF16) for w in (w_branch_a, w_branch_b, w_out, w_up, w_down))
    b_s_t = jnp.swapaxes(b_spatial, 1, 2)
    g_mix3, g_sgu3, g_ffn3 = (g.reshape(depth, 1, -1) for g in (g_mix, g_sgu, g_ffn))
    conv_b3 = conv_b.reshape(depth, 1, -1)

    xs = x.reshape(batch * seq, d)
    for l in range(depth):
        proj, logf = _in_proj(xs, g_mix3, w_main, wf_hi, wf_lo, b_f, l)
        q_aug, k_aug = _fox_prep(logf, proj, batch=batch, heads=heads)
        y_a = _fox_attn(q_aug, k_aug, proj, batch=batch, heads=heads)
        y_b = _sgu(proj, g_sgu3, w_spatial, b_s_t, l, u_col=u0 // sgu_w)
        merged = _merge(y_a, y_b, w_a_b, w_b_b, proj, l, gate_col=gate0)
        xs, h_ffn = _out_proj(merged, w_out_b, xs, g_ffn3, l)
        g = _ffn_up(h_ffn, w_up_b, conv_w, conv_b3, l, seq=seq)
        xs = _ffn_down(g, w_down_b, xs, l)
    return _final_norm(xs, g_final.reshape(1, d)).reshape(batch, seq, d)
```

```python
import functools
import math

import jax
import jax.numpy as jnp
from jax import lax
from jax.experimental import pallas as pl
from jax.experimental.pallas import tpu as pltpu

F32 = jnp.float32
BF16 = jnp.bfloat16

HEAD_DIM = 128
SGU_GROUP_DIM = 128
SGU_CHUNK = 128
CONV_WIDTH = 3
RMS_EPS = 1e-6
LOG2E = math.log2(math.e)

LANES = 128
AUG_DIM = 2 * HEAD_DIM
HALO_ROWS = 8
VMEM_LIMIT = 56 * 1024 * 1024


def _dot(a, b):
    return jnp.dot(a, b, preferred_element_type=F32)


def _gelu(x):
    a = -2.0 * LOG2E * math.sqrt(2.0 / math.pi)
    return x / (1.0 + jnp.exp2(x * (a + (a * 0.044715) * (x * x))))


def _sigmoid(x):
    return 1.0 / (1.0 + jnp.exp2(-LOG2E * x))


def _rms_scale(x, gain):
    inv = lax.rsqrt(jnp.mean(x * x, axis=-1, keepdims=True) + RMS_EPS)
    return (x * inv) * gain


def _split3(x):
    hi = x.astype(BF16)
    r = x - hi.astype(F32)
    mid = r.astype(BF16)
    lo = (r - mid.astype(F32)).astype(BF16)
    return hi, mid, lo


def _params(*sem):
    return pltpu.CompilerParams(dimension_semantics=sem, vmem_limit_bytes=VMEM_LIMIT)


def _stage_kernel(w_ref, nxt_ref, o_ref, wf_ref, *, first_shifted, gap):
    t = pl.program_id(1)

    @pl.when(t < first_shifted)
    def _():
        o_ref[...] = w_ref[...].T.astype(BF16)

    @pl.when(t >= first_shifted)
    def _():
        o_ref[...] = jnp.concatenate([w_ref[gap:, :], nxt_ref[...]], axis=0).T.astype(BF16)

    @pl.when(t == first_shifted)
    def _():
        row = lax.broadcasted_iota(jnp.int32, (LANES, w_ref.shape[1]), 0)
        w_f = jnp.where(row < gap, w_ref[:LANES, :], 0.0).T
        hi = w_f.astype(BF16)
        wf_ref[:, :LANES] = hi
        wf_ref[:, LANES:] = (w_f - hi.astype(F32)).astype(BF16)


def _stage_w_in(w_in_t, *, f0, gap, tn=1024):
    depth, n_in, d = w_in_t.shape
    n = n_in - gap
    first_shifted = f0 // tn
    assert gap == 8, "the column gap must be one f32 sublane tile"
    return pl.pallas_call(
        functools.partial(_stage_kernel, first_shifted=first_shifted, gap=gap),
        grid=(depth, n // tn),
        in_specs=[
            pl.BlockSpec((None, tn, d), lambda l, t: (l, t, 0)),
            pl.BlockSpec((None, gap, d), lambda l, t: (l, (t + 1) * (tn // gap), 0)),
        ],
        out_specs=[
            pl.BlockSpec((None, d, tn), lambda l, t: (l, 0, t)),
            pl.BlockSpec((None, d, 2 * LANES), lambda l, t: (l, 0, 0)),
        ],
        out_shape=[
            jax.ShapeDtypeStruct((depth, d, n), BF16),
            jax.ShapeDtypeStruct((depth, d, 2 * LANES), BF16),
        ],
        compiler_params=_params("parallel", "arbitrary"),
        name="stage_w_in",
    )(w_in_t, w_in_t)


def _inproj_kernel(x_ref, g_ref, w_ref, wf_ref, bf_ref, o_ref, lf_ref, h_ref, *, rows):
    @pl.when(pl.program_id(1) == 0)
    def _():
        def chunk(c, carry):
            sl = pl.ds(pl.multiple_of(c * rows, rows), rows)
            hb = _rms_scale(x_ref[sl, :], g_ref[...]).astype(BF16)
            h_ref[sl, :] = hb
            zz = _dot(hb, wf_ref[...])
            z = (zz[:, :LANES] + zz[:, LANES:]) + bf_ref[...]
            lf_ref[sl, :] = jnp.minimum(z, 0.0) - jnp.log(1.0 + jnp.exp(-jnp.abs(z)))
            return carry
        lax.fori_loop(0, x_ref.shape[0] // rows, chunk, 0)

    o_ref[...] = _dot(h_ref[...], w_ref[...]).astype(o_ref.dtype)


def _in_proj(x, gain, w, w_f, b_f, layer, *, tm=1024, tn=1536, rows=256):
    m, d = x.shape
    n = w.shape[2]
    return pl.pallas_call(
        functools.partial(_inproj_kernel, rows=rows),
        grid=(m // tm, n // tn),
        in_specs=[
            pl.BlockSpec((tm, d), lambda i, j: (i, 0)),
            pl.BlockSpec((None, 1, d), lambda i, j: (layer, 0, 0)),
            pl.BlockSpec((None, d, tn), lambda i, j: (layer, 0, j)),
            pl.BlockSpec((None, d, 2 * LANES), lambda i, j: (layer, 0, 0)),
            pl.BlockSpec((None, 1, LANES), lambda i, j: (layer, 0, 0)),
        ],
        out_specs=[
            pl.BlockSpec((tm, tn), lambda i, j: (i, j)),
            pl.BlockSpec((tm, LANES), lambda i, j: (i, 0)),
        ],
        out_shape=[
            jax.ShapeDtypeStruct((m, n), BF16),
            jax.ShapeDtypeStruct((m, LANES), F32),
        ],
        scratch_shapes=[pltpu.VMEM((tm, d), BF16)],
        compiler_params=_params("parallel", "arbitrary"),
        name="in_proj",
    )(x, gain, w, w_f, b_f)


def _decay_kernel(lf_ref, o_ref, carry_ref, *, heads):
    ts = lf_ref.shape[0]

    @pl.when(pl.program_id(1) == 0)
    def _():
        carry_ref[...] = jnp.zeros_like(carry_ref)

    r = lax.broadcasted_iota(jnp.int32, (ts, ts), 0)
    c = lax.broadcasted_iota(jnp.int32, (ts, ts), 1)
    tri = jnp.where(r >= c, 1.0, 0.0).astype(BF16)
    x1, x2, x3 = _split3(lf_ref[...])
    cum = (_dot(tri, x1) + _dot(tri, x2)) + _dot(tri, x3) + carry_ref[0:1, :]
    carry_ref[0:1, :] = cum[ts - 1:ts, :]

    lane = lax.broadcasted_iota(jnp.int32, cum.shape, 1)
    hi, mid, lo = _split3(jnp.where(lane < heads, cum * LOG2E, 0.0))
    packed = (hi.astype(F32) + pltpu.roll(mid.astype(F32), heads, 1)
              + pltpu.roll(lo.astype(F32), 2 * heads, 1))
    o_ref[...] = packed.astype(BF16)


def _fox_decay(logf, *, batch, heads, ts=512):
    m = logf.shape[0]
    nblk = m // batch // ts
    assert 3 * heads <= LANES
    return pl.pallas_call(
        functools.partial(_decay_kernel, heads=heads),
        grid=(batch, nblk),
        in_specs=[pl.BlockSpec((ts, LANES), lambda b, s: (b * nblk + s, 0))],
        out_specs=pl.BlockSpec((ts, LANES), lambda b, s: (b * nblk + s, 0)),
        out_shape=jax.ShapeDtypeStruct((m, LANES), BF16),
        scratch_shapes=[pltpu.VMEM((8, LANES), F32)],
        compiler_params=_params("parallel", "arbitrary"),
        name="fox_decay",
    )(logf)


def _decay_columns(parts, head0, hg, heads, lane0, sign):
    row = lax.broadcasted_iota(jnp.int32, (LANES, hg * LANES), 0)
    col = lax.broadcasted_iota(jnp.int32, (LANES, hg * LANES), 1)
    term, head = row // heads, row % heads
    hit = (term < 3) & (head == head0 + col // LANES) & (col % LANES == term + lane0)
    return _dot(parts, jnp.where(hit, sign, 0.0).astype(BF16))


def _attn_kernel(q_ref, k_ref, v_ref, dq_ref, dk_ref, o_ref, qa_ref, ka_ref, acc_ref, m_ref, s_ref,
                 *, hg, heads, tq, tk, q_scale, rows):
    i = pl.program_id(2)
    seq = k_ref.shape[0]
    head0 = pl.program_id(1) * hg
    lane_q = lax.broadcasted_iota(jnp.int32, (tq, LANES), 1)
    lane_k = lax.broadcasted_iota(jnp.int32, (rows, LANES), 1)

    @pl.when(i == 0)
    def _():
        def chunk(c, carry):
            sl = pl.ds(pl.multiple_of(c * rows, rows), rows)
            dk = _decay_columns(dk_ref[sl, :], head0, hg, heads, 3, -1.0)
            for h in range(hg):
                ext = jnp.where(lane_k < 3, 1.0, dk[:, h * LANES:(h + 1) * LANES])
                ka_ref[sl, h * AUG_DIM:h * AUG_DIM + HEAD_DIM] = k_ref[sl, h * HEAD_DIM:(h + 1) * HEAD_DIM]
                ka_ref[sl, h * AUG_DIM + HEAD_DIM:(h + 1) * AUG_DIM] = ext.astype(BF16)
            return carry
        lax.fori_loop(0, seq // rows, chunk, 0)

    dq = _decay_columns(dq_ref[...], head0, hg, heads, 0, 1.0)
    for h in range(hg):
        ext = jnp.where((lane_q >= 3) & (lane_q < 6), 1.0, dq[:, h * LANES:(h + 1) * LANES])
        qh = q_ref[:, h * HEAD_DIM:(h + 1) * HEAD_DIM].astype(F32) * q_scale
        qa_ref[:, h * AUG_DIM:h * AUG_DIM + HEAD_DIM] = qh.astype(BF16)
        qa_ref[:, h * AUG_DIM + HEAD_DIM:(h + 1) * AUG_DIM] = ext.astype(BF16)

    lane = lax.broadcasted_iota(jnp.int32, (tk, HEAD_DIM), 1)
    ones_col = jnp.where(lane == 0, 1.0, 0.0).astype(BF16)

    def scores(h, j0, r0):
        q = qa_ref[r0:, h * AUG_DIM:(h + 1) * AUG_DIM]
        k = ka_ref[pl.ds(j0, tk), h * AUG_DIM:(h + 1) * AUG_DIM]
        return lax.dot_general(q, k, (((1,), (1,)), ((), ())), preferred_element_type=F32)

    def fold(h, s, j0, r0, masked):
        v = v_ref[pl.ds(j0, tk), h * HEAD_DIM:(h + 1) * HEAD_DIM]
        if masked:
            qpos = i * tq + r0 + lax.broadcasted_iota(jnp.int32, s.shape, 0)
            kpos = j0 + lax.broadcasted_iota(jnp.int32, s.shape, 1)
            s = jnp.where(kpos <= qpos, s, -1e30)
        m_prev = m_ref[h, r0:, :]
        m_new = jnp.maximum(m_prev, jnp.max(s, axis=-1, keepdims=True))
        m_ref[h, r0:, :] = m_new
        p = jnp.exp2(s - jnp.tile(m_new, (1, tk // LANES))).astype(BF16)
        pv = _dot(p, jnp.concatenate([v, ones_col], axis=-1))
        alpha = jnp.tile(jnp.exp2(m_prev - m_new), (1, AUG_DIM // LANES))
        acc_ref[h, r0:, :] = acc_ref[h, r0:, :] * alpha + pv

    m_ref[...] = jnp.full(m_ref.shape, -1e30, F32)
    acc_ref[...] = jnp.zeros(acc_ref.shape, F32)

    for h in range(hg):
        s_ref[h] = scores(h, 0, 0)

    def body(j, carry):
        j0 = pl.multiple_of(j * tk, tk)
        for h in range(hg):
            s = s_ref[h]
            s_ref[h] = scores(h, j0 + tk, 0)
            fold(h, s, j0, 0, False)
        return carry

    lax.fori_loop(0, i * (tq // tk), body, 0)
    nd = tq // tk
    for d in range(nd):
        j0 = pl.multiple_of(i * tq + d * tk, tk)
        r0 = d * tk
        for h in range(hg):
            s = s_ref[h, r0:, :]
            if d + 1 < nd:
                s_ref[h, r0 + tk:, :] = scores(h, j0 + tk, r0 + tk)
            fold(h, s, j0, r0, True)
    for h in range(hg):
        acc = acc_ref[h]
        out = acc[:, :HEAD_DIM] / acc[:, HEAD_DIM:HEAD_DIM + 1]
        o_ref[:, h * HEAD_DIM:(h + 1) * HEAD_DIM] = out.astype(o_ref.dtype)


def _fox_attn(proj, decay, *, batch, heads, hg=2, tq=1024, tk=512, rows=512):
    m = proj.shape[0]
    seq = m // batch
    nq = seq // tq
    ng = heads // hg
    return pl.pallas_call(
        functools.partial(_attn_kernel, hg=hg, heads=heads, tq=tq, tk=tk,
                          q_scale=HEAD_DIM ** -0.5 * LOG2E, rows=rows),
        grid=(batch, ng, nq),
        in_specs=[
            pl.BlockSpec((tq, hg * HEAD_DIM), lambda b, g, i: (b * nq + i, g)),
            pl.BlockSpec((seq, hg * HEAD_DIM), lambda b, g, i: (b, ng + g)),
            pl.BlockSpec((seq, hg * HEAD_DIM), lambda b, g, i: (b, 2 * ng + g)),
            pl.BlockSpec((tq, LANES), lambda b, g, i: (b * nq + i, 0)),
            pl.BlockSpec((seq, LANES), lambda b, g, i: (b, 0)),
        ],
        out_specs=pl.BlockSpec((tq, hg * HEAD_DIM), lambda b, g, i: (b * nq + i, g)),
        out_shape=jax.ShapeDtypeStruct((m, heads * HEAD_DIM), BF16),
        scratch_shapes=[
            pltpu.VMEM((tq, hg * AUG_DIM), BF16),
            pltpu.VMEM((seq, hg * AUG_DIM), BF16),
            pltpu.VMEM((hg, tq, AUG_DIM), F32),
            pltpu.VMEM((hg, tq, LANES), F32),
            pltpu.VMEM((hg, tq, tk), F32),
        ],
        compiler_params=_params("parallel", "parallel", "arbitrary"),
        name="fox_attn",
    )(proj, proj, proj, decay, decay)


def _sgu_kernel(u_ref, v_ref, g_ref, w_ref, bt_ref, o_ref, *, groups):
    ts = u_ref.shape[0]
    vn = _rms_scale(_gelu(v_ref[...].astype(F32)), g_ref[...]).astype(BF16)
    r = lax.broadcasted_iota(jnp.int32, (SGU_CHUNK, SGU_CHUNK), 0)
    c = lax.broadcasted_iota(jnp.int32, (SGU_CHUNK, SGU_CHUNK), 1)
    for g in range(groups):
        w = jnp.where(r >= c, w_ref[g], 0.0).astype(BF16)
        bias = bt_ref[:, g:g + 1]
        cols = slice(g * SGU_GROUP_DIM, (g + 1) * SGU_GROUP_DIM)
        for ch in range(ts // SGU_CHUNK):
            rows = slice(ch * SGU_CHUNK, (ch + 1) * SGU_CHUNK)
            mixed = _dot(w, vn[rows, cols]) + bias
            o_ref[rows, cols] = (_gelu(u_ref[rows, cols].astype(F32)) * mixed).astype(o_ref.dtype)


def _sgu(proj, gain, w_s, b_s_t, layer, *, u_col, ts=256):
    m = proj.shape[0]
    groups = w_s.shape[1]
    width = groups * SGU_GROUP_DIM
    return pl.pallas_call(
        functools.partial(_sgu_kernel, groups=groups),
        grid=(m // ts,),
        in_specs=[
            pl.BlockSpec((ts, width), lambda i: (i, u_col)),
            pl.BlockSpec((ts, width), lambda i: (i, u_col + 1)),
            pl.BlockSpec((None, 1, width), lambda i: (layer, 0, 0)),
            pl.BlockSpec((None, groups, SGU_CHUNK, SGU_CHUNK), lambda i: (layer, 0, 0, 0)),
            pl.BlockSpec((None, SGU_CHUNK, groups), lambda i: (layer, 0, 0)),
        ],
        out_specs=pl.BlockSpec((ts, width), lambda i: (i, 0)),
        out_shape=jax.ShapeDtypeStruct((m, width), BF16),
        compiler_params=_params("parallel"),
        name="sgu",
    )(proj, proj, gain, w_s, b_s_t)


def _merge_kernel(ya_ref, yb_ref, wa_ref, wb_ref, ga_ref, gb_ref, o_ref):
    a = _sigmoid(ga_ref[...].astype(F32)) * _dot(ya_ref[...], wa_ref[...])
    b = _sigmoid(gb_ref[...].astype(F32)) * _dot(yb_ref[...], wb_ref[...])
    o_ref[...] = (a + b).astype(o_ref.dtype)


def _merge(y_a, y_b, w_a, w_b, proj, layer, *, gate_col, tm=1024, tn=1024):
    m, ka = y_a.shape
    kb = y_b.shape[1]
    n = w_a.shape[2]
    nb = n // tn
    ga0, gb0 = gate_col // tn, gate_col // tn + nb
    return pl.pallas_call(
        _merge_kernel,
        grid=(m // tm, nb),
        in_specs=[
            pl.BlockSpec((tm, ka), lambda i, j: (i, 0)),
            pl.BlockSpec((tm, kb), lambda i, j: (i, 0)),
            pl.BlockSpec((None, ka, tn), lambda i, j: (layer, 0, j)),
            pl.BlockSpec((None, kb, tn), lambda i, j: (layer, 0, j)),
            pl.BlockSpec((tm, tn), lambda i, j: (i, ga0 + j)),
            pl.BlockSpec((tm, tn), lambda i, j: (i, gb0 + j)),
        ],
        out_specs=pl.BlockSpec((tm, tn), lambda i, j: (i, j)),
        out_shape=jax.ShapeDtypeStruct((m, n), BF16),
        compiler_params=_params("parallel", "parallel"),
        name="merge",
    )(y_a, y_b, w_a, w_b, proj, proj)


def _outproj_kernel(a_ref, w_ref, x_ref, g_ref, xo_ref, h_ref):
    x_new = x_ref[...] + _dot(a_ref[...], w_ref[...])
    xo_ref[...] = x_new
    h_ref[...] = _rms_scale(x_new, g_ref[...]).astype(h_ref.dtype)


def _out_proj(a, w, x, gain, layer, *, tm=512):
    m, k = a.shape
    n = w.shape[2]
    return pl.pallas_call(
        _outproj_kernel,
        grid=(m // tm,),
        in_specs=[
            pl.BlockSpec((tm, k), lambda i: (i, 0)),
            pl.BlockSpec((None, k, n), lambda i: (layer, 0, 0)),
            pl.BlockSpec((tm, n), lambda i: (i, 0)),
            pl.BlockSpec((None, 1, n), lambda i: (layer, 0, 0)),
        ],
        out_specs=[pl.BlockSpec((tm, n), lambda i: (i, 0)), pl.BlockSpec((tm, n), lambda i: (i, 0))],
        out_shape=[jax.ShapeDtypeStruct((m, n), F32), jax.ShapeDtypeStruct((m, n), BF16)],
        compiler_params=_params("parallel"),
        name="out_proj",
    )(a, w, x, gain)


def _ffn_up_kernel(h_ref, wa_ref, wb_ref, cw_ref, cb_ref, o_ref, a_ref, tail_ref, *, tiles_per_seq, rows):
    tm = h_ref.shape[0]
    j = pl.program_id(1)
    seq_start = pl.program_id(0) % tiles_per_seq == 0

    @pl.when(seq_start)
    def _():
        a_ref[0:HALO_ROWS, :] = jnp.zeros((HALO_ROWS, a_ref.shape[1]), F32)

    @pl.when(jnp.logical_not(seq_start))
    def _():
        a_ref[0:HALO_ROWS, :] = tail_ref[j]

    for r0 in range(0, tm, rows):
        h = h_ref[r0:r0 + rows, :]
        a = _dot(h, wa_ref[...])
        a_ref[HALO_ROWS + r0:HALO_ROWS + r0 + rows, :] = a
        conv = (cb_ref[...] + cw_ref[0:1, :] * a_ref[pl.ds(HALO_ROWS - 2 + r0, rows), :]
                + cw_ref[1:2, :] * a_ref[pl.ds(HALO_ROWS - 1 + r0, rows), :] + cw_ref[2:3, :] * a)
        o_ref[r0:r0 + rows, :] = (_gelu(conv) * _dot(h, wb_ref[...])).astype(o_ref.dtype)
    tail_ref[j] = a_ref[tm:, :]


def _ffn_up(h, w_up, conv_w, conv_b, layer, *, seq, tm=1024, tn=512, rows=1024):
    m, d = h.shape
    dff = w_up.shape[2] // 2
    nb = dff // tn
    return pl.pallas_call(
        functools.partial(_ffn_up_kernel, tiles_per_seq=seq // tm, rows=rows),
        grid=(m // tm, nb),
        in_specs=[
            pl.BlockSpec((tm, d), lambda i, j: (i, 0)),
            pl.BlockSpec((None, d, tn), lambda i, j: (layer, 0, j)),
            pl.BlockSpec((None, d, tn), lambda i, j: (layer, 0, nb + j)),
            pl.BlockSpec((None, CONV_WIDTH, tn), lambda i, j: (layer, 0, j)),
            pl.BlockSpec((None, 1, tn), lambda i, j: (layer, 0, j)),
        ],
        out_specs=pl.BlockSpec((tm, tn), lambda i, j: (i, j)),
        out_shape=jax.ShapeDtypeStruct((m, dff), BF16),
        scratch_shapes=[pltpu.VMEM((HALO_ROWS + tm, tn), F32), pltpu.VMEM((nb, HALO_ROWS, tn), F32)],
        compiler_params=_params("arbitrary", "arbitrary"),
        name="ffn_up",
    )(h, w_up, w_up, conv_w, conv_b)


def _ffn_down_kernel(a_ref, w_ref, x_ref, o_ref):
    o_ref[...] = x_ref[...] + _dot(a_ref[...], w_ref[...])


def _ffn_down(a, w, x, layer, *, tm=1024, tn=512):
    m, k = a.shape
    n = w.shape[2]
    return pl.pallas_call(
        _ffn_down_kernel,
        grid=(m // tm, n // tn),
        in_specs=[
            pl.BlockSpec((tm, k), lambda i, j: (i, 0)),
            pl.BlockSpec((None, k, tn), lambda i, j: (layer, 0, j)),
            pl.BlockSpec((tm, tn), lambda i, j: (i, j)),
        ],
        out_specs=pl.BlockSpec((tm, tn), lambda i, j: (i, j)),
        out_shape=jax.ShapeDtypeStruct((m, n), F32),
        compiler_params=_params("parallel", "parallel"),
        name="ffn_down",
    )(a, w, x)


def _norm_kernel(x_ref, g_ref, o_ref):
    o_ref[...] = _rms_scale(x_ref[...], g_ref[...])


def _final_norm(x, gain, *, tm=256):
    m, d = x.shape
    return pl.pallas_call(
        _norm_kernel,
        grid=(m // tm,),
        in_specs=[pl.BlockSpec((tm, d), lambda i: (i, 0)), pl.BlockSpec((1, d), lambda i: (0, 0))],
        out_specs=pl.BlockSpec((tm, d), lambda i: (i, 0)),
        out_shape=jax.ShapeDtypeStruct((m, d), F32),
        compiler_params=_params("parallel"),
        name="final_norm",
    )(x, gain)


def kernel(x, g_mix, w_in, b_forget, g_sgu, w_spatial, b_spatial, w_branch_a, w_branch_b, w_out, g_ffn,
           w_up, conv_w, conv_b, w_down, g_final):
    batch, seq, d = x.shape
    depth = w_in.shape[0]
    heads = b_forget.shape[1]
    fox = heads * HEAD_DIM
    sgu_w = g_sgu.shape[1]
    f0 = 3 * fox
    u0 = 3 * fox
    gate0 = u0 + 2 * sgu_w
    assert fox == sgu_w, "column-block indexing of proj assumes equal branch widths"

    w_main, w_f = _stage_w_in(jnp.swapaxes(w_in, 1, 2), f0=f0, gap=heads)
    b_f = jnp.pad(b_forget, ((0, 0), (0, LANES - heads))).reshape(depth, 1, LANES)
    w_a_b, w_b_b, w_out_b, w_up_b, w_down_b = (
        w.astype(BF16) for w in (w_branch_a, w_branch_b, w_out, w_up, w_down))
    b_s_t = jnp.swapaxes(b_spatial, 1, 2)
    g_mix3, g_sgu3, g_ffn3 = (g.reshape(depth, 1, -1) for g in (g_mix, g_sgu, g_ffn))
    conv_b3 = conv_b.reshape(depth, 1, -1)

    xs = x.reshape(batch * seq, d)
    for l in range(depth):
        proj, logf = _in_proj(xs, g_mix3, w_main, w_f, b_f, l)
        decay = _fox_decay(logf, batch=batch, heads=heads)
        y_a = _fox_attn(proj, decay, batch=batch, heads=heads)
        y_b = _sgu(proj, g_sgu3, w_spatial, b_s_t, l, u_col=u0 // sgu_w)
        merged = _merge(y_a, y_b, w_a_b, w_b_b, proj, l, gate_col=gate0)
        xs, h_ffn = _out_proj(merged, w_out_b, xs, g_ffn3, l)
        g = _ffn_up(h_ffn, w_up_b, conv_w, conv_b3, l, seq=seq)
        xs = _ffn_down(g, w_down_b, xs, l)
    return _final_norm(xs, g_final.reshape(1, d)).reshape(batch, seq, d)
```

```python
import functools
import math

import jax
import jax.numpy as jnp
from jax import lax
from jax.experimental import pallas as pl
from jax.experimental.pallas import tpu as pltpu

F32 = jnp.float32
BF16 = jnp.bfloat16

HEAD_DIM = 128
SGU_GROUP_DIM = 128
SGU_CHUNK = 128
CONV_WIDTH = 3
RMS_EPS = 1e-6
LOG2E = math.log2(math.e)

LANES = 128
AUG_DIM = 2 * HEAD_DIM
HALO_ROWS = 8
VMEM_LIMIT = 56 * 1024 * 1024


def _dot(a, b):
    return jnp.dot(a, b, preferred_element_type=F32)


def _gelu(x):
    a = -2.0 * LOG2E * math.sqrt(2.0 / math.pi)
    return x / (1.0 + jnp.exp2(x * (a + (a * 0.044715) * (x * x))))


def _sigmoid(x):
    return 1.0 / (1.0 + jnp.exp2(-LOG2E * x))


def _rms_scale(x, gain):
    inv = lax.rsqrt(jnp.mean(x * x, axis=-1, keepdims=True) + RMS_EPS)
    return (x * inv) * gain


def _split3(x):
    hi = x.astype(BF16)
    r = x - hi.astype(F32)
    mid = r.astype(BF16)
    lo = (r - mid.astype(F32)).astype(BF16)
    return hi, mid, lo


def _params(*sem):
    return pltpu.CompilerParams(dimension_semantics=sem, vmem_limit_bytes=VMEM_LIMIT)


def _stage_kernel(w_ref, nxt_ref, o_ref, wf_ref, *, first_shifted, gap):
    t = pl.program_id(1)

    @pl.when(t < first_shifted)
    def _():
        o_ref[...] = w_ref[...].T.astype(BF16)

    @pl.when(t >= first_shifted)
    def _():
        o_ref[...] = jnp.concatenate([w_ref[gap:, :], nxt_ref[...]], axis=0).T.astype(BF16)

    @pl.when(t == first_shifted)
    def _():
        row = lax.broadcasted_iota(jnp.int32, (LANES, w_ref.shape[1]), 0)
        w_f = jnp.where(row < gap, w_ref[:LANES, :], 0.0).T
        hi = w_f.astype(BF16)
        wf_ref[:, :LANES] = hi
        wf_ref[:, LANES:] = (w_f - hi.astype(F32)).astype(BF16)


def _stage_w_in(w_in_t, *, f0, gap, tn=1024):
    depth, n_in, d = w_in_t.shape
    n = n_in - gap
    first_shifted = f0 // tn
    assert gap == 8, "the column gap must be one f32 sublane tile"
    return pl.pallas_call(
        functools.partial(_stage_kernel, first_shifted=first_shifted, gap=gap),
        grid=(depth, n // tn),
        in_specs=[
            pl.BlockSpec((None, tn, d), lambda l, t: (l, t, 0)),
            pl.BlockSpec((None, gap, d), lambda l, t: (l, (t + 1) * (tn // gap), 0)),
        ],
        out_specs=[
            pl.BlockSpec((None, d, tn), lambda l, t: (l, 0, t)),
            pl.BlockSpec((None, d, 2 * LANES), lambda l, t: (l, 0, 0)),
        ],
        out_shape=[
            jax.ShapeDtypeStruct((depth, d, n), BF16),
            jax.ShapeDtypeStruct((depth, d, 2 * LANES), BF16),
        ],
        compiler_params=_params("parallel", "arbitrary"),
        name="stage_w_in",
    )(w_in_t, w_in_t)


def _inproj_kernel(x_ref, g_ref, w_ref, wf_ref, bf_ref, o_ref, lf_ref, h_ref, *, rows):
    j = pl.program_id(1)

    @pl.when(j == 0)
    def _():
        for r0 in range(0, x_ref.shape[0], rows):
            sl = slice(r0, r0 + rows)
            hb = _rms_scale(x_ref[sl, :], g_ref[...]).astype(BF16)
            h_ref[sl, :] = hb
            zz = _dot(hb, wf_ref[...])
            z = (zz[:, :LANES] + zz[:, LANES:]) + bf_ref[...]
            lf_ref[sl, :] = jnp.minimum(z, 0.0) - jnp.log(1.0 + jnp.exp(-jnp.abs(z)))
            o_ref[sl, :] = _dot(hb, w_ref[...]).astype(o_ref.dtype)

    @pl.when(j > 0)
    def _():
        o_ref[...] = _dot(h_ref[...], w_ref[...]).astype(o_ref.dtype)


def _in_proj(x, gain, w, w_f, b_f, layer, *, tm=1024, tn=1536, rows=256):
    m, d = x.shape
    n = w.shape[2]
    return pl.pallas_call(
        functools.partial(_inproj_kernel, rows=rows),
        grid=(m // tm, n // tn),
        in_specs=[
            pl.BlockSpec((tm, d), lambda i, j: (i, 0)),
            pl.BlockSpec((None, 1, d), lambda i, j: (layer, 0, 0)),
            pl.BlockSpec((None, d, tn), lambda i, j: (layer, 0, j)),
            pl.BlockSpec((None, d, 2 * LANES), lambda i, j: (layer, 0, 0)),
            pl.BlockSpec((None, 1, LANES), lambda i, j: (layer, 0, 0)),
        ],
        out_specs=[
            pl.BlockSpec((tm, tn), lambda i, j: (i, j)),
            pl.BlockSpec((tm, LANES), lambda i, j: (i, 0)),
        ],
        out_shape=[
            jax.ShapeDtypeStruct((m, n), BF16),
            jax.ShapeDtypeStruct((m, LANES), F32),
        ],
        scratch_shapes=[pltpu.VMEM((tm, d), BF16)],
        compiler_params=_params("parallel", "arbitrary"),
        name="in_proj",
    )(x, gain, w, w_f, b_f)


def _decay_kernel(lf_ref, o_ref, carry_ref, *, heads):
    ts = lf_ref.shape[0]

    @pl.when(pl.program_id(1) == 0)
    def _():
        carry_ref[...] = jnp.zeros_like(carry_ref)

    r = lax.broadcasted_iota(jnp.int32, (ts, ts), 0)
    c = lax.broadcasted_iota(jnp.int32, (ts, ts), 1)
    tri = jnp.where(r >= c, 1.0, 0.0).astype(BF16)
    x1, x2, x3 = _split3(lf_ref[...])
    cum = (_dot(tri, x1) + _dot(tri, x2)) + _dot(tri, x3) + carry_ref[0:1, :]
    carry_ref[0:1, :] = cum[ts - 1:ts, :]

    lane = lax.broadcasted_iota(jnp.int32, cum.shape, 1)
    hi, mid, lo = _split3(jnp.where(lane < heads, cum * LOG2E, 0.0))
    packed = (hi.astype(F32) + pltpu.roll(mid.astype(F32), heads, 1)
              + pltpu.roll(lo.astype(F32), 2 * heads, 1))
    o_ref[...] = packed.astype(BF16)


def _fox_decay(logf, *, batch, heads, ts=512):
    m = logf.shape[0]
    nblk = m // batch // ts
    assert 3 * heads <= LANES
    return pl.pallas_call(
        functools.partial(_decay_kernel, heads=heads),
        grid=(batch, nblk),
        in_specs=[pl.BlockSpec((ts, LANES), lambda b, s: (b * nblk + s, 0))],
        out_specs=pl.BlockSpec((ts, LANES), lambda b, s: (b * nblk + s, 0)),
        out_shape=jax.ShapeDtypeStruct((m, LANES), BF16),
        scratch_shapes=[pltpu.VMEM((8, LANES), F32)],
        compiler_params=_params("parallel", "arbitrary"),
        name="fox_decay",
    )(logf)


def _decay_columns(parts, head0, hg, heads, lane0, sign):
    row = lax.broadcasted_iota(jnp.int32, (LANES, hg * LANES), 0)
    col = lax.broadcasted_iota(jnp.int32, (LANES, hg * LANES), 1)
    term, head = row // heads, row % heads
    hit = (term < 3) & (head == head0 + col // LANES) & (col % LANES == term + lane0)
    return _dot(parts, jnp.where(hit, sign, 0.0).astype(BF16))


def _attn_kernel(q_ref, k_ref, v_ref, dq_ref, dk_ref, o_ref, qa_ref, ka_ref, acc_ref, m_ref, s_ref,
                 *, hg, heads, tq, tk, q_scale, rows):
    i = pl.program_id(2)
    seq = k_ref.shape[0]
    head0 = pl.program_id(1) * hg
    lane_q = lax.broadcasted_iota(jnp.int32, (tq, LANES), 1)
    lane_k = lax.broadcasted_iota(jnp.int32, (rows, LANES), 1)

    @pl.when(i == 0)
    def _():
        def chunk(c, carry):
            sl = pl.ds(pl.multiple_of(c * rows, rows), rows)
            dk = _decay_columns(dk_ref[sl, :], head0, hg, heads, 3, -1.0)
            for h in range(hg):
                ext = jnp.where(lane_k < 3, 1.0, dk[:, h * LANES:(h + 1) * LANES])
                ka_ref[sl, h * AUG_DIM:h * AUG_DIM + HEAD_DIM] = k_ref[sl, h * HEAD_DIM:(h + 1) * HEAD_DIM]
                ka_ref[sl, h * AUG_DIM + HEAD_DIM:(h + 1) * AUG_DIM] = ext.astype(BF16)
            return carry
        lax.fori_loop(0, seq // rows, chunk, 0)

    dq = _decay_columns(dq_ref[...], head0, hg, heads, 0, 1.0)
    for h in range(hg):
        ext = jnp.where((lane_q >= 3) & (lane_q < 6), 1.0, dq[:, h * LANES:(h + 1) * LANES])
        qh = q_ref[:, h * HEAD_DIM:(h + 1) * HEAD_DIM].astype(F32) * q_scale
        qa_ref[:, h * AUG_DIM:h * AUG_DIM + HEAD_DIM] = qh.astype(BF16)
        qa_ref[:, h * AUG_DIM + HEAD_DIM:(h + 1) * AUG_DIM] = ext.astype(BF16)

    lane = lax.broadcasted_iota(jnp.int32, (tk, HEAD_DIM), 1)
    ones_col = jnp.where(lane == 0, 1.0, 0.0).astype(BF16)

    def scores(h, j0, r0):
        q = qa_ref[r0:, h * AUG_DIM:(h + 1) * AUG_DIM]
        k = ka_ref[pl.ds(j0, tk), h * AUG_DIM:(h + 1) * AUG_DIM]
        return lax.dot_general(q, k, (((1,), (1,)), ((), ())), preferred_element_type=F32)

    def fold(h, s, j0, r0, masked):
        v = v_ref[pl.ds(j0, tk), h * HEAD_DIM:(h + 1) * HEAD_DIM]
        if masked:
            qpos = i * tq + r0 + lax.broadcasted_iota(jnp.int32, s.shape, 0)
            kpos = j0 + lax.broadcasted_iota(jnp.int32, s.shape, 1)
            s = jnp.where(kpos <= qpos, s, -1e30)
        m_prev = m_ref[h, r0:, :]
        m_new = jnp.maximum(m_prev, jnp.max(s, axis=-1, keepdims=True))
        m_ref[h, r0:, :] = m_new
        p = jnp.exp2(s - jnp.tile(m_new, (1, tk // LANES))).astype(BF16)
        pv = _dot(p, jnp.concatenate([v, ones_col], axis=-1))
        alpha = jnp.tile(jnp.exp2(m_prev - m_new), (1, AUG_DIM // LANES))
        acc_ref[h, r0:, :] = acc_ref[h, r0:, :] * alpha + pv

    m_ref[...] = jnp.full(m_ref.shape, -1e30, F32)
    acc_ref[...] = jnp.zeros(acc_ref.shape, F32)

    for h in range(hg):
        s_ref[h] = scores(h, 0, 0)

    def body(j, carry):
        j0 = pl.multiple_of(j * tk, tk)
        for h in range(hg):
            s = s_ref[h]
            s_ref[h] = scores(h, j0 + tk, 0)
            fold(h, s, j0, 0, False)
        return carry

    lax.fori_loop(0, i * (tq // tk), body, 0)
    nd = tq // tk
    for d in range(nd):
        j0 = pl.multiple_of(i * tq + d * tk, tk)
        r0 = d * tk
        for h in range(hg):
            s = s_ref[h, r0:, :]
            if d + 1 < nd:
                s_ref[h, r0 + tk:, :] = scores(h, j0 + tk, r0 + tk)
            fold(h, s, j0, r0, True)
    for h in range(hg):
        acc = acc_ref[h]
        out = acc[:, :HEAD_DIM] / acc[:, HEAD_DIM:HEAD_DIM + 1]
        o_ref[:, h * HEAD_DIM:(h + 1) * HEAD_DIM] = out.astype(o_ref.dtype)


def _fox_attn(proj, decay, *, batch, heads, hg=2, tq=1024, tk=512, rows=512):
    m = proj.shape[0]
    seq = m // batch
    nq = seq // tq
    ng = heads // hg
    return pl.pallas_call(
        functools.partial(_attn_kernel, hg=hg, heads=heads, tq=tq, tk=tk,
                          q_scale=HEAD_DIM ** -0.5 * LOG2E, rows=rows),
        grid=(batch, ng, nq),
        in_specs=[
            pl.BlockSpec((tq, hg * HEAD_DIM), lambda b, g, i: (b * nq + i, g)),
            pl.BlockSpec((seq, hg * HEAD_DIM), lambda b, g, i: (b, ng + g)),
            pl.BlockSpec((seq, hg * HEAD_DIM), lambda b, g, i: (b, 2 * ng + g)),
            pl.BlockSpec((tq, LANES), lambda b, g, i: (b * nq + i, 0)),
            pl.BlockSpec((seq, LANES), lambda b, g, i: (b, 0)),
        ],
        out_specs=pl.BlockSpec((tq, hg * HEAD_DIM), lambda b, g, i: (b * nq + i, g)),
        out_shape=jax.ShapeDtypeStruct((m, heads * HEAD_DIM), BF16),
        scratch_shapes=[
            pltpu.VMEM((tq, hg * AUG_DIM), BF16),
            pltpu.VMEM((seq, hg * AUG_DIM), BF16),
            pltpu.VMEM((hg, tq, AUG_DIM), F32),
            pltpu.VMEM((hg, tq, LANES), F32),
            pltpu.VMEM((hg, tq, tk), F32),
        ],
        compiler_params=_params("parallel", "parallel", "arbitrary"),
        name="fox_attn",
    )(proj, proj, proj, decay, decay)


def _sgu_kernel(u_ref, v_ref, g_ref, w_ref, bt_ref, o_ref, *, groups):
    ts = u_ref.shape[0]
    vn = _rms_scale(_gelu(v_ref[...].astype(F32)), g_ref[...]).astype(BF16)
    r = lax.broadcasted_iota(jnp.int32, (SGU_CHUNK, SGU_CHUNK), 0)
    c = lax.broadcasted_iota(jnp.int32, (SGU_CHUNK, SGU_CHUNK), 1)
    for g in range(groups):
        w = jnp.where(r >= c, w_ref[g], 0.0).astype(BF16)
        bias = bt_ref[:, g:g + 1]
        cols = slice(g * SGU_GROUP_DIM, (g + 1) * SGU_GROUP_DIM)
        for ch in range(ts // SGU_CHUNK):
            rows = slice(ch * SGU_CHUNK, (ch + 1) * SGU_CHUNK)
            mixed = _dot(w, vn[rows, cols]) + bias
            o_ref[rows, cols] = (_gelu(u_ref[rows, cols].astype(F32)) * mixed).astype(o_ref.dtype)


def _sgu(proj, gain, w_s, b_s_t, layer, *, u_col, ts=256):
    m = proj.shape[0]
    groups = w_s.shape[1]
    width = groups * SGU_GROUP_DIM
    return pl.pallas_call(
        functools.partial(_sgu_kernel, groups=groups),
        grid=(m // ts,),
        in_specs=[
            pl.BlockSpec((ts, width), lambda i: (i, u_col)),
            pl.BlockSpec((ts, width), lambda i: (i, u_col + 1)),
            pl.BlockSpec((None, 1, width), lambda i: (layer, 0, 0)),
            pl.BlockSpec((None, groups, SGU_CHUNK, SGU_CHUNK), lambda i: (layer, 0, 0, 0)),
            pl.BlockSpec((None, SGU_CHUNK, groups), lambda i: (layer, 0, 0)),
        ],
        out_specs=pl.BlockSpec((ts, width), lambda i: (i, 0)),
        out_shape=jax.ShapeDtypeStruct((m, width), BF16),
        compiler_params=_params("parallel"),
        name="sgu",
    )(proj, proj, gain, w_s, b_s_t)


def _merge_kernel(ya_ref, yb_ref, wa_ref, wb_ref, ga_ref, gb_ref, o_ref):
    a = _sigmoid(ga_ref[...].astype(F32)) * _dot(ya_ref[...], wa_ref[...])
    b = _sigmoid(gb_ref[...].astype(F32)) * _dot(yb_ref[...], wb_ref[...])
    o_ref[...] = (a + b).astype(o_ref.dtype)


def _merge(y_a, y_b, w_a, w_b, proj, layer, *, gate_col, tm=1024, tn=1024):
    m, ka = y_a.shape
    kb = y_b.shape[1]
    n = w_a.shape[2]
    nb = n // tn
    ga0, gb0 = gate_col // tn, gate_col // tn + nb
    return pl.pallas_call(
        _merge_kernel,
        grid=(m // tm, nb),
        in_specs=[
            pl.BlockSpec((tm, ka), lambda i, j: (i, 0)),
            pl.BlockSpec((tm, kb), lambda i, j: (i, 0)),
            pl.BlockSpec((None, ka, tn), lambda i, j: (layer, 0, j)),
            pl.BlockSpec((None, kb, tn), lambda i, j: (layer, 0, j)),
            pl.BlockSpec((tm, tn), lambda i, j: (i, ga0 + j)),
            pl.BlockSpec((tm, tn), lambda i, j: (i, gb0 + j)),
        ],
        out_specs=pl.BlockSpec((tm, tn), lambda i, j: (i, j)),
        out_shape=jax.ShapeDtypeStruct((m, n), BF16),
        compiler_params=_params("parallel", "parallel"),
        name="merge",
    )(y_a, y_b, w_a, w_b, proj, proj)


def _outproj_kernel(a_ref, w_ref, x_ref, g_ref, xo_ref, h_ref):
    x_new = x_ref[...] + _dot(a_ref[...], w_ref[...])
    xo_ref[...] = x_new
    h_ref[...] = _rms_scale(x_new, g_ref[...]).astype(h_ref.dtype)


def _out_proj(a, w, x, gain, layer, *, tm=512):
    m, k = a.shape
    n = w.shape[2]
    return pl.pallas_call(
        _outproj_kernel,
        grid=(m // tm,),
        in_specs=[
            pl.BlockSpec((tm, k), lambda i: (i, 0)),
            pl.BlockSpec((None, k, n), lambda i: (layer, 0, 0)),
            pl.BlockSpec((tm, n), lambda i: (i, 0)),
            pl.BlockSpec((None, 1, n), lambda i: (layer, 0, 0)),
        ],
        out_specs=[pl.BlockSpec((tm, n), lambda i: (i, 0)), pl.BlockSpec((tm, n), lambda i: (i, 0))],
        out_shape=[jax.ShapeDtypeStruct((m, n), F32), jax.ShapeDtypeStruct((m, n), BF16)],
        compiler_params=_params("parallel"),
        name="out_proj",
    )(a, w, x, gain)


def _ffn_up_kernel(h_ref, wa_ref, wb_ref, cw_ref, cb_ref, o_ref, wa_s, wb_s, a_ref, tail_ref, *, tiles_per_seq):
    tm = h_ref.shape[0]
    i = pl.program_id(1)

    @pl.when(i == 0)
    def _():
        wa_s[...] = wa_ref[...].astype(BF16)
        wb_s[...] = wb_ref[...].astype(BF16)

    seq_start = i % tiles_per_seq == 0

    @pl.when(seq_start)
    def _():
        a_ref[0:HALO_ROWS, :] = jnp.zeros((HALO_ROWS, a_ref.shape[1]), F32)

    @pl.when(jnp.logical_not(seq_start))
    def _():
        a_ref[0:HALO_ROWS, :] = tail_ref[...]

    h = h_ref[...]
    a = _dot(h, wa_s[...])
    a_ref[HALO_ROWS:, :] = a
    conv = (cb_ref[...] + cw_ref[0:1, :] * a_ref[pl.ds(HALO_ROWS - 2, tm), :]
            + cw_ref[1:2, :] * a_ref[pl.ds(HALO_ROWS - 1, tm), :] + cw_ref[2:3, :] * a)
    o_ref[...] = (_gelu(conv) * _dot(h, wb_s[...])).astype(o_ref.dtype)
    tail_ref[...] = a_ref[tm:, :]


def _ffn_up(h, w_up, conv_w, conv_b, layer, *, seq, tm=1024, tn=512):
    m, d = h.shape
    dff = w_up.shape[2] // 2
    nb = dff // tn
    return pl.pallas_call(
        functools.partial(_ffn_up_kernel, tiles_per_seq=seq // tm),
        grid=(nb, m // tm),
        in_specs=[
            pl.BlockSpec((tm, d), lambda j, i: (i, 0)),
            pl.BlockSpec((None, d, tn), lambda j, i: (layer, 0, j)),
            pl.BlockSpec((None, d, tn), lambda j, i: (layer, 0, nb + j)),
            pl.BlockSpec((None, CONV_WIDTH, tn), lambda j, i: (layer, 0, j)),
            pl.BlockSpec((None, 1, tn), lambda j, i: (layer, 0, j)),
        ],
        out_specs=pl.BlockSpec((tm, tn), lambda j, i: (i, j)),
        out_shape=jax.ShapeDtypeStruct((m, dff), BF16),
        scratch_shapes=[
            pltpu.VMEM((d, tn), BF16),
            pltpu.VMEM((d, tn), BF16),
            pltpu.VMEM((HALO_ROWS + tm, tn), F32),
            pltpu.VMEM((HALO_ROWS, tn), F32),
        ],
        compiler_params=_params("parallel", "arbitrary"),
        name="ffn_up",
    )(h, w_up, w_up, conv_w, conv_b)


def _ffn_down_kernel(a_ref, w_ref, x_ref, o_ref):
    o_ref[...] = x_ref[...] + _dot(a_ref[...], w_ref[...])


def _ffn_down(a, w, x, layer, *, tm=1024, tn=512):
    m, k = a.shape
    n = w.shape[2]
    return pl.pallas_call(
        _ffn_down_kernel,
        grid=(m // tm, n // tn),
        in_specs=[
            pl.BlockSpec((tm, k), lambda i, j: (i, 0)),
            pl.BlockSpec((None, k, tn), lambda i, j: (layer, 0, j)),
            pl.BlockSpec((tm, tn), lambda i, j: (i, j)),
        ],
        out_specs=pl.BlockSpec((tm, tn), lambda i, j: (i, j)),
        out_shape=jax.ShapeDtypeStruct((m, n), F32),
        compiler_params=_params("parallel", "parallel"),
        name="ffn_down",
    )(a, w, x)


def _norm_kernel(x_ref, g_ref, o_ref):
    o_ref[...] = _rms_scale(x_ref[...], g_ref[...])


def _final_norm(x, gain, *, tm=256):
    m, d = x.shape
    return pl.pallas_call(
        _norm_kernel,
        grid=(m // tm,),
        in_specs=[pl.BlockSpec((tm, d), lambda i: (i, 0)), pl.BlockSpec((1, d), lambda i: (0, 0))],
        out_specs=pl.BlockSpec((tm, d), lambda i: (i, 0)),
        out_shape=jax.ShapeDtypeStruct((m, d), F32),
        compiler_params=_params("parallel"),
        name="final_norm",
    )(x, gain)


def kernel(x, g_mix, w_in, b_forget, g_sgu, w_spatial, b_spatial, w_branch_a, w_branch_b, w_out, g_ffn,
           w_up, conv_w, conv_b, w_down, g_final):
    batch, seq, d = x.shape
    depth = w_in.shape[0]
    heads = b_forget.shape[1]
    fox = heads * HEAD_DIM
    sgu_w = g_sgu.shape[1]
    f0 = 3 * fox
    u0 = 3 * fox
    gate0 = u0 + 2 * sgu_w
    assert fox == sgu_w, "column-block indexing of proj assumes equal branch widths"

    w_main, w_f = _stage_w_in(jnp.swapaxes(w_in, 1, 2), f0=f0, gap=heads)
    b_f = jnp.pad(b_forget, ((0, 0), (0, LANES - heads))).reshape(depth, 1, LANES)
    w_a_b, w_b_b, w_out_b, w_down_b = (w.astype(BF16) for w in (w_branch_a, w_branch_b, w_out, w_down))
    b_s_t = jnp.swapaxes(b_spatial, 1, 2)
    g_mix3, g_sgu3, g_ffn3 = (g.reshape(depth, 1, -1) for g in (g_mix, g_sgu, g_ffn))
    conv_b3 = conv_b.reshape(depth, 1, -1)

    xs = x.reshape(batch * seq, d)
    for l in range(depth):
        proj, logf = _in_proj(xs, g_mix3, w_main, w_f, b_f, l)
        decay = _fox_decay(logf, batch=batch, heads=heads)
        y_a = _fox_attn(proj, decay, batch=batch, heads=heads)
        y_b = _sgu(proj, g_sgu3, w_spatial, b_s_t, l, u_col=u0 // sgu_w)
        merged = _merge(y_a, y_b, w_a_b, w_b_b, proj, l, gate_col=gate0)
        xs, h_ffn = _out_proj(merged, w_out_b, xs, g_ffn3, l)
        g = _ffn_up(h_ffn, w_up, conv_w, conv_b3, l, seq=seq)
        xs = _ffn_down(g, w_down_b, xs, l)
    return _final_norm(xs, g_final.reshape(1, d)).reshape(batch, seq, d)
```

```python
import functools
import math

import jax
import jax.numpy as jnp
from jax import lax
from jax.experimental import pallas as pl
from jax.experimental.pallas import tpu as pltpu

F32 = jnp.float32
BF16 = jnp.bfloat16

HEAD_DIM = 128
SGU_GROUP_DIM = 128
SGU_CHUNK = 128
CONV_WIDTH = 3
RMS_EPS = 1e-6
LOG2E = math.log2(math.e)

LANES = 128
AUG_DIM = 2 * HEAD_DIM
HALO_ROWS = 8
VMEM_LIMIT = 56 * 1024 * 1024


def _dot(a, b):
    return jnp.dot(a, b, preferred_element_type=F32)


def _gelu(x):
    a = -2.0 * LOG2E * math.sqrt(2.0 / math.pi)
    return x / (1.0 + jnp.exp2(x * (a + (a * 0.044715) * (x * x))))


def _sigmoid(x):
    return 1.0 / (1.0 + jnp.exp2(-LOG2E * x))


def _rms_scale(x, gain):
    inv = lax.rsqrt(jnp.mean(x * x, axis=-1, keepdims=True) + RMS_EPS)
    return (x * inv) * gain


def _split3(x):
    hi = x.astype(BF16)
    r = x - hi.astype(F32)
    mid = r.astype(BF16)
    lo = (r - mid.astype(F32)).astype(BF16)
    return hi, mid, lo


def _params(*sem):
    return pltpu.CompilerParams(dimension_semantics=sem, vmem_limit_bytes=VMEM_LIMIT)


def _stage_kernel(w_ref, nxt_ref, o_ref, wf_ref, *, first_shifted, gap):
    t = pl.program_id(1)

    @pl.when(t < first_shifted)
    def _():
        o_ref[...] = w_ref[...].T.astype(BF16)

    @pl.when(t >= first_shifted)
    def _():
        o_ref[...] = jnp.concatenate([w_ref[gap:, :], nxt_ref[...]], axis=0).T.astype(BF16)

    @pl.when(t == first_shifted)
    def _():
        row = lax.broadcasted_iota(jnp.int32, (LANES, w_ref.shape[1]), 0)
        w_f = jnp.where(row < gap, w_ref[:LANES, :], 0.0).T
        hi = w_f.astype(BF16)
        wf_ref[:, :LANES] = hi
        wf_ref[:, LANES:] = (w_f - hi.astype(F32)).astype(BF16)


def _stage_w_in(w_in_t, *, f0, gap, tn=1024):
    depth, n_in, d = w_in_t.shape
    n = n_in - gap
    first_shifted = f0 // tn
    assert gap == 8, "the column gap must be one f32 sublane tile"
    return pl.pallas_call(
        functools.partial(_stage_kernel, first_shifted=first_shifted, gap=gap),
        grid=(depth, n // tn),
        in_specs=[
            pl.BlockSpec((None, tn, d), lambda l, t: (l, t, 0)),
            pl.BlockSpec((None, gap, d), lambda l, t: (l, (t + 1) * (tn // gap), 0)),
        ],
        out_specs=[
            pl.BlockSpec((None, d, tn), lambda l, t: (l, 0, t)),
            pl.BlockSpec((None, d, 2 * LANES), lambda l, t: (l, 0, 0)),
        ],
        out_shape=[
            jax.ShapeDtypeStruct((depth, d, n), BF16),
            jax.ShapeDtypeStruct((depth, d, 2 * LANES), BF16),
        ],
        compiler_params=_params("parallel", "arbitrary"),
        name="stage_w_in",
    )(w_in_t, w_in_t)


def _inproj_kernel(x_ref, g_ref, w_ref, wf_ref, bf_ref, o_ref, lf_ref, h_ref, *, rows):
    j = pl.program_id(1)

    @pl.when(j == 0)
    def _():
        for r0 in range(0, x_ref.shape[0], rows):
            sl = slice(r0, r0 + rows)
            hb = _rms_scale(x_ref[sl, :], g_ref[...]).astype(BF16)
            h_ref[sl, :] = hb
            zz = _dot(hb, wf_ref[...])
            z = (zz[:, :LANES] + zz[:, LANES:]) + bf_ref[...]
            lf_ref[sl, :] = jnp.minimum(z, 0.0) - jnp.log(1.0 + jnp.exp(-jnp.abs(z)))
            o_ref[sl, :] = _dot(hb, w_ref[...]).astype(o_ref.dtype)

    @pl.when(j > 0)
    def _():
        o_ref[...] = _dot(h_ref[...], w_ref[...]).astype(o_ref.dtype)


def _in_proj(x, gain, w, w_f, b_f, layer, *, tm=1024, tn=1536, rows=256):
    m, d = x.shape
    n = w.shape[2]
    return pl.pallas_call(
        functools.partial(_inproj_kernel, rows=rows),
        grid=(m // tm, n // tn),
        in_specs=[
            pl.BlockSpec((tm, d), lambda i, j: (i, 0)),
            pl.BlockSpec((None, 1, d), lambda i, j: (layer, 0, 0)),
            pl.BlockSpec((None, d, tn), lambda i, j: (layer, 0, j)),
            pl.BlockSpec((None, d, 2 * LANES), lambda i, j: (layer, 0, 0)),
            pl.BlockSpec((None, 1, LANES), lambda i, j: (layer, 0, 0)),
        ],
        out_specs=[
            pl.BlockSpec((tm, tn), lambda i, j: (i, j)),
            pl.BlockSpec((tm, LANES), lambda i, j: (i, 0)),
        ],
        out_shape=[
            jax.ShapeDtypeStruct((m, n), BF16),
            jax.ShapeDtypeStruct((m, LANES), F32),
        ],
        scratch_shapes=[pltpu.VMEM((tm, d), BF16)],
        compiler_params=_params("parallel", "arbitrary"),
        name="in_proj",
    )(x, gain, w, w_f, b_f)


def _decay_kernel(lf_ref, o_ref, carry_ref, *, heads):
    ts = lf_ref.shape[0]

    @pl.when(pl.program_id(1) == 0)
    def _():
        carry_ref[...] = jnp.zeros_like(carry_ref)

    r = lax.broadcasted_iota(jnp.int32, (ts, ts), 0)
    c = lax.broadcasted_iota(jnp.int32, (ts, ts), 1)
    tri = jnp.where(r >= c, 1.0, 0.0).astype(BF16)
    x1, x2, x3 = _split3(lf_ref[...])
    cum = (_dot(tri, x1) + _dot(tri, x2)) + _dot(tri, x3) + carry_ref[0:1, :]
    carry_ref[0:1, :] = cum[ts - 1:ts, :]

    lane = lax.broadcasted_iota(jnp.int32, cum.shape, 1)
    hi, mid, lo = _split3(jnp.where(lane < heads, cum * LOG2E, 0.0))
    packed = (hi.astype(F32) + pltpu.roll(mid.astype(F32), heads, 1)
              + pltpu.roll(lo.astype(F32), 2 * heads, 1))
    o_ref[...] = packed.astype(BF16)


def _fox_decay(logf, *, batch, heads, ts=512):
    m = logf.shape[0]
    nblk = m // batch // ts
    assert 3 * heads <= LANES
    return pl.pallas_call(
        functools.partial(_decay_kernel, heads=heads),
        grid=(batch, nblk),
        in_specs=[pl.BlockSpec((ts, LANES), lambda b, s: (b * nblk + s, 0))],
        out_specs=pl.BlockSpec((ts, LANES), lambda b, s: (b * nblk + s, 0)),
        out_shape=jax.ShapeDtypeStruct((m, LANES), BF16),
        scratch_shapes=[pltpu.VMEM((8, LANES), F32)],
        compiler_params=_params("parallel", "arbitrary"),
        name="fox_decay",
    )(logf)


def _decay_columns(parts, head0, hg, heads, lane0, sign):
    row = lax.broadcasted_iota(jnp.int32, (LANES, hg * LANES), 0)
    col = lax.broadcasted_iota(jnp.int32, (LANES, hg * LANES), 1)
    term, head = row // heads, row % heads
    hit = (term < 3) & (head == head0 + col // LANES) & (col % LANES == term + lane0)
    return _dot(parts, jnp.where(hit, sign, 0.0).astype(BF16))


def _attn_kernel(q_ref, k_ref, v_ref, dq_ref, dk_ref, o_ref, qa_ref, ka_ref, acc_ref, m_ref, s_ref,
                 *, hg, heads, tq, tk, q_scale, rows):
    i = pl.program_id(2)
    seq = k_ref.shape[0]
    head0 = pl.program_id(1) * hg
    lane_q = lax.broadcasted_iota(jnp.int32, (tq, LANES), 1)
    lane_k = lax.broadcasted_iota(jnp.int32, (rows, LANES), 1)

    @pl.when(i == 0)
    def _():
        def chunk(c, carry):
            sl = pl.ds(pl.multiple_of(c * rows, rows), rows)
            dk = _decay_columns(dk_ref[sl, :], head0, hg, heads, 3, -1.0)
            for h in range(hg):
                ext = jnp.where(lane_k < 3, 1.0, dk[:, h * LANES:(h + 1) * LANES])
                ka_ref[sl, h * AUG_DIM:h * AUG_DIM + HEAD_DIM] = k_ref[sl, h * HEAD_DIM:(h + 1) * HEAD_DIM]
                ka_ref[sl, h * AUG_DIM + HEAD_DIM:(h + 1) * AUG_DIM] = ext.astype(BF16)
            return carry
        lax.fori_loop(0, seq // rows, chunk, 0)

    dq = _decay_columns(dq_ref[...], head0, hg, heads, 0, 1.0)
    for h in range(hg):
        ext = jnp.where((lane_q >= 3) & (lane_q < 6), 1.0, dq[:, h * LANES:(h + 1) * LANES])
        qh = q_ref[:, h * HEAD_DIM:(h + 1) * HEAD_DIM].astype(F32) * q_scale
        qa_ref[:, h * AUG_DIM:h * AUG_DIM + HEAD_DIM] = qh.astype(BF16)
        qa_ref[:, h * AUG_DIM + HEAD_DIM:(h + 1) * AUG_DIM] = ext.astype(BF16)

    lane = lax.broadcasted_iota(jnp.int32, (tk, HEAD_DIM), 1)
    ones_col = jnp.where(lane == 0, 1.0, 0.0).astype(BF16)
    causal_bias = jnp.where(lax.broadcasted_iota(jnp.int32, (tk, tk), 1)
                            <= lax.broadcasted_iota(jnp.int32, (tk, tk), 0), 0.0, -1e30)

    def scores(h, j0, r0):
        q = qa_ref[r0:, h * AUG_DIM:(h + 1) * AUG_DIM]
        k = ka_ref[pl.ds(j0, tk), h * AUG_DIM:(h + 1) * AUG_DIM]
        return lax.dot_general(q, k, (((1,), (1,)), ((), ())), preferred_element_type=F32)

    def fold(h, s, j0, r0, masked):
        v = v_ref[pl.ds(j0, tk), h * HEAD_DIM:(h + 1) * HEAD_DIM]
        if masked:
            top = s[:tk] + causal_bias
            s = top if s.shape[0] == tk else jnp.concatenate([top, s[tk:]], axis=0)
        m_prev = m_ref[h, r0:, :]
        m_new = jnp.maximum(m_prev, jnp.max(s, axis=-1, keepdims=True))
        m_ref[h, r0:, :] = m_new
        p = jnp.exp2(s - jnp.tile(m_new, (1, tk // LANES))).astype(BF16)
        pv = _dot(p, jnp.concatenate([v, ones_col], axis=-1))
        alpha = jnp.tile(jnp.exp2(m_prev - m_new), (1, AUG_DIM // LANES))
        acc_ref[h, r0:, :] = acc_ref[h, r0:, :] * alpha + pv

    for h in range(hg):
        s_ref[h] = scores(h, 0, 0)
    m_ref[...] = jnp.full(m_ref.shape, -1e30, F32)
    acc_ref[...] = jnp.zeros(acc_ref.shape, F32)

    def body(j, carry):
        j0 = pl.multiple_of(j * tk, tk)
        for h in range(hg):
            s = s_ref[h]
            s_ref[h] = scores(h, j0 + tk, 0)
            fold(h, s, j0, 0, False)
        return carry

    lax.fori_loop(0, i * (tq // tk), body, 0)
    nd = tq // tk
    for d in range(nd):
        j0 = pl.multiple_of(i * tq + d * tk, tk)
        r0 = d * tk
        for h in range(hg):
            s = s_ref[h, r0:, :]
            if d + 1 < nd:
                s_ref[h, r0 + tk:, :] = scores(h, j0 + tk, r0 + tk)
            fold(h, s, j0, r0, True)
    for h in range(hg):
        acc = acc_ref[h]
        out = acc[:, :HEAD_DIM] / acc[:, HEAD_DIM:HEAD_DIM + 1]
        o_ref[:, h * HEAD_DIM:(h + 1) * HEAD_DIM] = out.astype(o_ref.dtype)


def _fox_attn(proj, decay, *, batch, heads, hg=4, tq=1024, tk=512, rows=512):
    m = proj.shape[0]
    seq = m // batch
    nq = seq // tq
    ng = heads // hg
    return pl.pallas_call(
        functools.partial(_attn_kernel, hg=hg, heads=heads, tq=tq, tk=tk,
                          q_scale=HEAD_DIM ** -0.5 * LOG2E, rows=rows),
        grid=(batch, ng, nq),
        in_specs=[
            pl.BlockSpec((tq, hg * HEAD_DIM), lambda b, g, i: (b * nq + i, g)),
            pl.BlockSpec((seq, hg * HEAD_DIM), lambda b, g, i: (b, ng + g), pipeline_mode=pl.Buffered(1)),
            pl.BlockSpec((seq, hg * HEAD_DIM), lambda b, g, i: (b, 2 * ng + g)),
            pl.BlockSpec((tq, LANES), lambda b, g, i: (b * nq + i, 0)),
            pl.BlockSpec((seq, LANES), lambda b, g, i: (b, 0), pipeline_mode=pl.Buffered(1)),
        ],
        out_specs=pl.BlockSpec((tq, hg * HEAD_DIM), lambda b, g, i: (b * nq + i, g)),
        out_shape=jax.ShapeDtypeStruct((m, heads * HEAD_DIM), BF16),
        scratch_shapes=[
            pltpu.VMEM((tq, hg * AUG_DIM), BF16),
            pltpu.VMEM((seq, hg * AUG_DIM), BF16),
            pltpu.VMEM((hg, tq, AUG_DIM), F32),
            pltpu.VMEM((hg, tq, LANES), F32),
            pltpu.VMEM((hg, tq, tk), F32),
        ],
        compiler_params=_params("parallel", "parallel", "arbitrary"),
        name="fox_attn",
    )(proj, proj, proj, decay, decay)


def _sgu_kernel(u_ref, v_ref, g_ref, w_ref, bt_ref, o_ref, *, groups):
    ts = u_ref.shape[0]
    vn = _rms_scale(_gelu(v_ref[...].astype(F32)), g_ref[...]).astype(BF16)
    r = lax.broadcasted_iota(jnp.int32, (SGU_CHUNK, SGU_CHUNK), 0)
    c = lax.broadcasted_iota(jnp.int32, (SGU_CHUNK, SGU_CHUNK), 1)
    for g in range(groups):
        w = jnp.where(r >= c, w_ref[g], 0.0).astype(BF16)
        bias = bt_ref[:, g:g + 1]
        cols = slice(g * SGU_GROUP_DIM, (g + 1) * SGU_GROUP_DIM)
        for ch in range(ts // SGU_CHUNK):
            rows = slice(ch * SGU_CHUNK, (ch + 1) * SGU_CHUNK)
            mixed = _dot(w, vn[rows, cols]) + bias
            o_ref[rows, cols] = (_gelu(u_ref[rows, cols].astype(F32)) * mixed).astype(o_ref.dtype)


def _sgu(proj, gain, w_s, b_s_t, layer, *, u_col, ts=512):
    m = proj.shape[0]
    groups = w_s.shape[1]
    width = groups * SGU_GROUP_DIM
    return pl.pallas_call(
        functools.partial(_sgu_kernel, groups=groups),
        grid=(m // ts,),
        in_specs=[
            pl.BlockSpec((ts, width), lambda i: (i, u_col)),
            pl.BlockSpec((ts, width), lambda i: (i, u_col + 1)),
            pl.BlockSpec((None, 1, width), lambda i: (layer, 0, 0)),
            pl.BlockSpec((None, groups, SGU_CHUNK, SGU_CHUNK), lambda i: (layer, 0, 0, 0)),
            pl.BlockSpec((None, SGU_CHUNK, groups), lambda i: (layer, 0, 0)),
        ],
        out_specs=pl.BlockSpec((ts, width), lambda i: (i, 0)),
        out_shape=jax.ShapeDtypeStruct((m, width), BF16),
        compiler_params=_params("parallel"),
        name="sgu",
    )(proj, proj, gain, w_s, b_s_t)


def _merge_kernel(ya_ref, yb_ref, wa_ref, wb_ref, ga_ref, gb_ref, o_ref):
    a = _sigmoid(ga_ref[...].astype(F32)) * _dot(ya_ref[...], wa_ref[...])
    b = _sigmoid(gb_ref[...].astype(F32)) * _dot(yb_ref[...], wb_ref[...])
    o_ref[...] = (a + b).astype(o_ref.dtype)


def _merge(y_a, y_b, w_a, w_b, proj, layer, *, gate_col, tm=1024, tn=1024):
    m, ka = y_a.shape
    kb = y_b.shape[1]
    n = w_a.shape[2]
    nb = n // tn
    ga0, gb0 = gate_col // tn, gate_col // tn + nb
    return pl.pallas_call(
        _merge_kernel,
        grid=(m // tm, nb),
        in_specs=[
            pl.BlockSpec((tm, ka), lambda i, j: (i, 0)),
            pl.BlockSpec((tm, kb), lambda i, j: (i, 0)),
            pl.BlockSpec((None, ka, tn), lambda i, j: (layer, 0, j)),
            pl.BlockSpec((None, kb, tn), lambda i, j: (layer, 0, j)),
            pl.BlockSpec((tm, tn), lambda i, j: (i, ga0 + j)),
            pl.BlockSpec((tm, tn), lambda i, j: (i, gb0 + j)),
        ],
        out_specs=pl.BlockSpec((tm, tn), lambda i, j: (i, j)),
        out_shape=jax.ShapeDtypeStruct((m, n), BF16),
        compiler_params=_params("parallel", "parallel"),
        name="merge",
    )(y_a, y_b, w_a, w_b, proj, proj)


def _outproj_kernel(a_ref, w_ref, x_ref, g_ref, xo_ref, h_ref):
    x_new = x_ref[...] + _dot(a_ref[...], w_ref[...])
    xo_ref[...] = x_new
    h_ref[...] = _rms_scale(x_new, g_ref[...]).astype(h_ref.dtype)


def _out_proj(a, w, x, gain, layer, *, tm=512):
    m, k = a.shape
    n = w.shape[2]
    return pl.pallas_call(
        _outproj_kernel,
        grid=(m // tm,),
        in_specs=[
            pl.BlockSpec((tm, k), lambda i: (i, 0)),
            pl.BlockSpec((None, k, n), lambda i: (layer, 0, 0)),
            pl.BlockSpec((tm, n), lambda i: (i, 0)),
            pl.BlockSpec((None, 1, n), lambda i: (layer, 0, 0)),
        ],
        out_specs=[pl.BlockSpec((tm, n), lambda i: (i, 0)), pl.BlockSpec((tm, n), lambda i: (i, 0))],
        out_shape=[jax.ShapeDtypeStruct((m, n), F32), jax.ShapeDtypeStruct((m, n), BF16)],
        compiler_params=_params("parallel"),
        name="out_proj",
    )(a, w, x, gain)


def _ffn_up_kernel(h_ref, wa_ref, wb_ref, cw_ref, cb_ref, o_ref, wa_s, wb_s, a_ref, tail_ref, *, tiles_per_seq):
    tm = h_ref.shape[0]
    i = pl.program_id(1)

    @pl.when(i == 0)
    def _():
        wa_s[...] = wa_ref[...].astype(BF16)
        wb_s[...] = wb_ref[...].astype(BF16)

    seq_start = i % tiles_per_seq == 0

    @pl.when(seq_start)
    def _():
        a_ref[0:HALO_ROWS, :] = jnp.zeros((HALO_ROWS, a_ref.shape[1]), F32)

    @pl.when(jnp.logical_not(seq_start))
    def _():
        a_ref[0:HALO_ROWS, :] = tail_ref[...]

    h = h_ref[...]
    a = _dot(h, wa_s[...])
    a_ref[HALO_ROWS:, :] = a
    conv = (cb_ref[...] + cw_ref[0:1, :] * a_ref[pl.ds(HALO_ROWS - 2, tm), :]
            + cw_ref[1:2, :] * a_ref[pl.ds(HALO_ROWS - 1, tm), :] + cw_ref[2:3, :] * a)
    o_ref[...] = (_gelu(conv) * _dot(h, wb_s[...])).astype(o_ref.dtype)
    tail_ref[...] = a_ref[tm:, :]


def _ffn_up(h, w_up, conv_w, conv_b, layer, *, seq, tm=1024, tn=512):
    m, d = h.shape
    dff = w_up.shape[2] // 2
    nb = dff // tn
    return pl.pallas_call(
        functools.partial(_ffn_up_kernel, tiles_per_seq=seq // tm),
        grid=(nb, m // tm),
        in_specs=[
            pl.BlockSpec((tm, d), lambda j, i: (i, 0)),
            pl.BlockSpec((None, d, tn), lambda j, i: (layer, 0, j)),
            pl.BlockSpec((None, d, tn), lambda j, i: (layer, 0, nb + j)),
            pl.BlockSpec((None, CONV_WIDTH, tn), lambda j, i: (layer, 0, j)),
            pl.BlockSpec((None, 1, tn), lambda j, i: (layer, 0, j)),
        ],
        out_specs=pl.BlockSpec((tm, tn), lambda j, i: (i, j)),
        out_shape=jax.ShapeDtypeStruct((m, dff), BF16),
        scratch_shapes=[
            pltpu.VMEM((d, tn), BF16),
            pltpu.VMEM((d, tn), BF16),
            pltpu.VMEM((HALO_ROWS + tm, tn), F32),
            pltpu.VMEM((HALO_ROWS, tn), F32),
        ],
        compiler_params=_params("parallel", "arbitrary"),
        name="ffn_up",
    )(h, w_up, w_up, conv_w, conv_b)


def _ffn_down_kernel(a_ref, w_ref, x_ref, o_ref):
    o_ref[...] = x_ref[...] + _dot(a_ref[...], w_ref[...])


def _ffn_down(a, w, x, layer, *, tm=1024, tn=512):
    m, k = a.shape
    n = w.shape[2]
    return pl.pallas_call(
        _ffn_down_kernel,
        grid=(m // tm, n // tn),
        in_specs=[
            pl.BlockSpec((tm, k), lambda i, j: (i, 0)),
            pl.BlockSpec((None, k, tn), lambda i, j: (layer, 0, j)),
            pl.BlockSpec((tm, tn), lambda i, j: (i, j)),
        ],
        out_specs=pl.BlockSpec((tm, tn), lambda i, j: (i, j)),
        out_shape=jax.ShapeDtypeStruct((m, n), F32),
        compiler_params=_params("parallel", "parallel"),
        name="ffn_down",
    )(a, w, x)


def _norm_kernel(x_ref, g_ref, o_ref):
    o_ref[...] = _rms_scale(x_ref[...], g_ref[...])


def _final_norm(x, gain, *, tm=256):
    m, d = x.shape
    return pl.pallas_call(
        _norm_kernel,
        grid=(m // tm,),
        in_specs=[pl.BlockSpec((tm, d), lambda i: (i, 0)), pl.BlockSpec((1, d), lambda i: (0, 0))],
        out_specs=pl.BlockSpec((tm, d), lambda i: (i, 0)),
        out_shape=jax.ShapeDtypeStruct((m, d), F32),
        compiler_params=_params("parallel"),
        name="final_norm",
    )(x, gain)


def kernel(x, g_mix, w_in, b_forget, g_sgu, w_spatial, b_spatial, w_branch_a, w_branch_b, w_out, g_ffn,
           w_up, conv_w, conv_b, w_down, g_final):
    batch, seq, d = x.shape
    depth = w_in.shape[0]
    heads = b_forget.shape[1]
    fox = heads * HEAD_DIM
    sgu_w = g_sgu.shape[1]
    f0 = 3 * fox
    u0 = 3 * fox
    gate0 = u0 + 2 * sgu_w
    assert fox == sgu_w, "column-block indexing of proj assumes equal branch widths"

    w_main, w_f = _stage_w_in(jnp.swapaxes(w_in, 1, 2), f0=f0, gap=heads)
    b_f = jnp.pad(b_forget, ((0, 0), (0, LANES - heads))).reshape(depth, 1, LANES)
    w_a_b, w_b_b, w_out_b, w_down_b = (w.astype(BF16) for w in (w_branch_a, w_branch_b, w_out, w_down))
    b_s_t = jnp.swapaxes(b_spatial, 1, 2)
    g_mix3, g_sgu3, g_ffn3 = (g.reshape(depth, 1, -1) for g in (g_mix, g_sgu, g_ffn))
    conv_b3 = conv_b.reshape(depth, 1, -1)

    xs = x.reshape(batch * seq, d)
    for l in range(depth):
        proj, logf = _in_proj(xs, g_mix3, w_main, w_f, b_f, l)
        decay = _fox_decay(logf, batch=batch, heads=heads)
        y_a = _fox_attn(proj, decay, batch=batch, heads=heads)
        y_b = _sgu(proj, g_sgu3, w_spatial, b_s_t, l, u_col=u0 // sgu_w)
        merged = _merge(y_a, y_b, w_a_b, w_b_b, proj, l, gate_col=gate0)
        xs, h_ffn = _out_proj(merged, w_out_b, xs, g_ffn3, l)
        g = _ffn_up(h_ffn, w_up, conv_w, conv_b3, l, seq=seq)
        xs = _ffn_down(g, w_down_b, xs, l)
    return _final_norm(xs, g_final.reshape(1, d)).reshape(batch, seq, d)
```

```python
import functools
import math

import jax
import jax.numpy as jnp
from jax import lax
from jax.experimental import pallas as pl
from jax.experimental.pallas import tpu as pltpu

F32 = jnp.float32
BF16 = jnp.bfloat16

HEAD_DIM = 128
SGU_GROUP_DIM = 128
SGU_CHUNK = 128
CONV_WIDTH = 3
RMS_EPS = 1e-6
LOG2E = math.log2(math.e)

LANES = 128
AUG_DIM = 2 * HEAD_DIM
HALO_ROWS = 8
VMEM_LIMIT = 56 * 1024 * 1024


def _dot(a, b):
    return jnp.dot(a, b, preferred_element_type=F32)


def _gelu(x):
    a = -2.0 * LOG2E * math.sqrt(2.0 / math.pi)
    return x / (1.0 + jnp.exp2(x * (a + (a * 0.044715) * (x * x))))


def _sigmoid(x):
    return 1.0 / (1.0 + jnp.exp2(-LOG2E * x))


def _rms_scale(x, gain):
    inv = lax.rsqrt(jnp.mean(x * x, axis=-1, keepdims=True) + RMS_EPS)
    return (x * inv) * gain


def _split3(x):
    hi = x.astype(BF16)
    r = x - hi.astype(F32)
    mid = r.astype(BF16)
    lo = (r - mid.astype(F32)).astype(BF16)
    return hi, mid, lo


def _params(*sem):
    return pltpu.CompilerParams(dimension_semantics=sem, vmem_limit_bytes=VMEM_LIMIT)


def _stage_kernel(w_ref, nxt_ref, o_ref, wf_ref, *, first_shifted, gap):
    t = pl.program_id(1)

    @pl.when(t < first_shifted)
    def _():
        o_ref[...] = w_ref[...].T.astype(BF16)

    @pl.when(t >= first_shifted)
    def _():
        o_ref[...] = jnp.concatenate([w_ref[gap:, :], nxt_ref[...]], axis=0).T.astype(BF16)

    @pl.when(t == first_shifted)
    def _():
        row = lax.broadcasted_iota(jnp.int32, (LANES, w_ref.shape[1]), 0)
        w_f = jnp.where(row < gap, w_ref[:LANES, :], 0.0).T
        hi = w_f.astype(BF16)
        wf_ref[:, :LANES] = hi
        wf_ref[:, LANES:] = (w_f - hi.astype(F32)).astype(BF16)


def _stage_w_in(w_in_t, *, f0, gap, tn=1024):
    depth, n_in, d = w_in_t.shape
    n = n_in - gap
    first_shifted = f0 // tn
    assert gap == 8, "the column gap must be one f32 sublane tile"
    return pl.pallas_call(
        functools.partial(_stage_kernel, first_shifted=first_shifted, gap=gap),
        grid=(depth, n // tn),
        in_specs=[
            pl.BlockSpec((None, tn, d), lambda l, t: (l, t, 0)),
            pl.BlockSpec((None, gap, d), lambda l, t: (l, (t + 1) * (tn // gap), 0)),
        ],
        out_specs=[
            pl.BlockSpec((None, d, tn), lambda l, t: (l, 0, t)),
            pl.BlockSpec((None, d, 2 * LANES), lambda l, t: (l, 0, 0)),
        ],
        out_shape=[
            jax.ShapeDtypeStruct((depth, d, n), BF16),
            jax.ShapeDtypeStruct((depth, d, 2 * LANES), BF16),
        ],
        compiler_params=_params("parallel", "arbitrary"),
        name="stage_w_in",
    )(w_in_t, w_in_t)


def _inproj_kernel(x_ref, g_ref, w_ref, wf_ref, bf_ref, o_ref, lf_ref, h_ref, *, rows):
    j = pl.program_id(1)

    @pl.when(j == 0)
    def _():
        for r0 in range(0, x_ref.shape[0], rows):
            sl = slice(r0, r0 + rows)
            hb = _rms_scale(x_ref[sl, :], g_ref[...]).astype(BF16)
            h_ref[sl, :] = hb
            zz = _dot(hb, wf_ref[...])
            z = (zz[:, :LANES] + zz[:, LANES:]) + bf_ref[...]
            lf_ref[sl, :] = jnp.minimum(z, 0.0) - jnp.log(1.0 + jnp.exp(-jnp.abs(z)))
            o_ref[sl, :] = _dot(hb, w_ref[...]).astype(o_ref.dtype)

    @pl.when(j > 0)
    def _():
        o_ref[...] = _dot(h_ref[...], w_ref[...]).astype(o_ref.dtype)


def _in_proj(x, gain, w, w_f, b_f, layer, *, tm=1024, tn=1536, rows=256):
    m, d = x.shape
    n = w.shape[2]
    return pl.pallas_call(
        functools.partial(_inproj_kernel, rows=rows),
        grid=(m // tm, n // tn),
        in_specs=[
            pl.BlockSpec((tm, d), lambda i, j: (i, 0)),
            pl.BlockSpec((None, 1, d), lambda i, j: (layer, 0, 0)),
            pl.BlockSpec((None, d, tn), lambda i, j: (layer, 0, j)),
            pl.BlockSpec((None, d, 2 * LANES), lambda i, j: (layer, 0, 0)),
            pl.BlockSpec((None, 1, LANES), lambda i, j: (layer, 0, 0)),
        ],
        out_specs=[
            pl.BlockSpec((tm, tn), lambda i, j: (i, j)),
            pl.BlockSpec((tm, LANES), lambda i, j: (i, 0)),
        ],
        out_shape=[
            jax.ShapeDtypeStruct((m, n), BF16),
            jax.ShapeDtypeStruct((m, LANES), F32),
        ],
        scratch_shapes=[pltpu.VMEM((tm, d), BF16)],
        compiler_params=_params("parallel", "arbitrary"),
        name="in_proj",
    )(x, gain, w, w_f, b_f)


def _decay_kernel(lf_ref, o_ref, carry_ref, *, heads):
    ts = lf_ref.shape[0]

    @pl.when(pl.program_id(1) == 0)
    def _():
        carry_ref[...] = jnp.zeros_like(carry_ref)

    r = lax.broadcasted_iota(jnp.int32, (ts, ts), 0)
    c = lax.broadcasted_iota(jnp.int32, (ts, ts), 1)
    tri = jnp.where(r >= c, 1.0, 0.0).astype(BF16)
    x1, x2, x3 = _split3(lf_ref[...])
    cum = (_dot(tri, x1) + _dot(tri, x2)) + _dot(tri, x3) + carry_ref[0:1, :]
    carry_ref[0:1, :] = cum[ts - 1:ts, :]

    lane = lax.broadcasted_iota(jnp.int32, cum.shape, 1)
    hi, mid, lo = _split3(jnp.where(lane < heads, cum * LOG2E, 0.0))
    packed = (hi.astype(F32) + pltpu.roll(mid.astype(F32), heads, 1)
              + pltpu.roll(lo.astype(F32), 2 * heads, 1))
    o_ref[...] = packed.astype(BF16)


def _fox_decay(logf, *, batch, heads, ts=512):
    m = logf.shape[0]
    nblk = m // batch // ts
    assert 3 * heads <= LANES
    return pl.pallas_call(
        functools.partial(_decay_kernel, heads=heads),
        grid=(batch, nblk),
        in_specs=[pl.BlockSpec((ts, LANES), lambda b, s: (b * nblk + s, 0))],
        out_specs=pl.BlockSpec((ts, LANES), lambda b, s: (b * nblk + s, 0)),
        out_shape=jax.ShapeDtypeStruct((m, LANES), BF16),
        scratch_shapes=[pltpu.VMEM((8, LANES), F32)],
        compiler_params=_params("parallel", "arbitrary"),
        name="fox_decay",
    )(logf)


def _decay_columns(parts, head0, hg, heads, lane0, sign):
    row = lax.broadcasted_iota(jnp.int32, (LANES, hg * LANES), 0)
    col = lax.broadcasted_iota(jnp.int32, (LANES, hg * LANES), 1)
    term, head = row // heads, row % heads
    hit = (term < 3) & (head == head0 + col // LANES) & (col % LANES == term + lane0)
    return _dot(parts, jnp.where(hit, sign, 0.0).astype(BF16))


def _attn_kernel(q_ref, k_ref, v_ref, dq_ref, dk_ref, o_ref, qa_ref, ka_ref, acc_ref, m_ref, s_ref,
                 *, hg, heads, tq, tk, q_scale, rows):
    i = pl.program_id(2)
    seq = k_ref.shape[0]
    head0 = pl.program_id(1) * hg
    lane_q = lax.broadcasted_iota(jnp.int32, (tq, LANES), 1)
    lane_k = lax.broadcasted_iota(jnp.int32, (rows, LANES), 1)

    @pl.when(i == 0)
    def _():
        def chunk(c, carry):
            sl = pl.ds(pl.multiple_of(c * rows, rows), rows)
            dk = _decay_columns(dk_ref[sl, :], head0, hg, heads, 3, -1.0)
            for h in range(hg):
                ext = jnp.where(lane_k < 3, 1.0, dk[:, h * LANES:(h + 1) * LANES])
                ka_ref[sl, h * AUG_DIM:h * AUG_DIM + HEAD_DIM] = k_ref[sl, h * HEAD_DIM:(h + 1) * HEAD_DIM]
                ka_ref[sl, h * AUG_DIM + HEAD_DIM:(h + 1) * AUG_DIM] = ext.astype(BF16)
            return carry
        lax.fori_loop(0, seq // rows, chunk, 0)

    dq = _decay_columns(dq_ref[...], head0, hg, heads, 0, 1.0)
    for h in range(hg):
        ext = jnp.where((lane_q >= 3) & (lane_q < 6), 1.0, dq[:, h * LANES:(h + 1) * LANES])
        qh = q_ref[:, h * HEAD_DIM:(h + 1) * HEAD_DIM].astype(F32) * q_scale
        qa_ref[:, h * AUG_DIM:h * AUG_DIM + HEAD_DIM] = qh.astype(BF16)
        qa_ref[:, h * AUG_DIM + HEAD_DIM:(h + 1) * AUG_DIM] = ext.astype(BF16)

    lane = lax.broadcasted_iota(jnp.int32, (tk, HEAD_DIM), 1)
    ones_col = jnp.where(lane == 0, 1.0, 0.0).astype(BF16)
    causal_bias = jnp.where(lax.broadcasted_iota(jnp.int32, (tk, tk), 1)
                            <= lax.broadcasted_iota(jnp.int32, (tk, tk), 0), 0.0, -1e30)

    def scores(h, j0, r0):
        q = qa_ref[r0:, h * AUG_DIM:(h + 1) * AUG_DIM]
        k = ka_ref[pl.ds(j0, tk), h * AUG_DIM:(h + 1) * AUG_DIM]
        return lax.dot_general(q, k, (((1,), (1,)), ((), ())), preferred_element_type=F32)

    def fold(h, s, j0, r0, masked):
        v = v_ref[pl.ds(j0, tk), h * HEAD_DIM:(h + 1) * HEAD_DIM]
        if masked:
            top = s[:tk] + causal_bias
            s = top if s.shape[0] == tk else jnp.concatenate([top, s[tk:]], axis=0)
        m_prev = m_ref[h, r0:, :]
        m_new = jnp.maximum(m_prev, jnp.max(s, axis=-1, keepdims=True))
        m_ref[h, r0:, :] = m_new
        p = jnp.exp2(s - jnp.tile(m_new, (1, tk // LANES))).astype(BF16)
        pv = _dot(p, jnp.concatenate([v, ones_col], axis=-1))
        alpha = jnp.tile(jnp.exp2(m_prev - m_new), (1, AUG_DIM // LANES))
        acc_ref[h, r0:, :] = acc_ref[h, r0:, :] * alpha + pv

    for h in range(hg):
        s_ref[h] = scores(h, 0, 0)
    m_ref[...] = jnp.full(m_ref.shape, -1e30, F32)
    acc_ref[...] = jnp.zeros(acc_ref.shape, F32)

    def body(j, carry):
        j0 = pl.multiple_of(j * tk, tk)
        for h in range(hg):
            s = s_ref[h]
            s_ref[h] = scores(h, j0 + tk, 0)
            fold(h, s, j0, 0, False)
        return carry

    lax.fori_loop(0, i * (tq // tk), body, 0)
    nd = tq // tk
    for d in range(nd):
        j0 = pl.multiple_of(i * tq + d * tk, tk)
        r0 = d * tk
        for h in range(hg):
            s = s_ref[h, r0:, :]
            if d + 1 < nd:
                s_ref[h, r0 + tk:, :] = scores(h, j0 + tk, r0 + tk)
            fold(h, s, j0, r0, True)
    for h in range(hg):
        acc = acc_ref[h]
        out = acc[:, :HEAD_DIM] / acc[:, HEAD_DIM:HEAD_DIM + 1]
        o_ref[:, h * HEAD_DIM:(h + 1) * HEAD_DIM] = out.astype(o_ref.dtype)


def _fox_attn(proj, decay, *, batch, heads, hg=4, tq=1024, tk=512, rows=512):
    m = proj.shape[0]
    seq = m // batch
    nq = seq // tq
    ng = heads // hg
    return pl.pallas_call(
        functools.partial(_attn_kernel, hg=hg, heads=heads, tq=tq, tk=tk,
                          q_scale=HEAD_DIM ** -0.5 * LOG2E, rows=rows),
        grid=(batch, ng, nq),
        in_specs=[
            pl.BlockSpec((tq, hg * HEAD_DIM), lambda b, g, i: (b * nq + i, g)),
            pl.BlockSpec((seq, hg * HEAD_DIM), lambda b, g, i: (b, ng + g), pipeline_mode=pl.Buffered(1)),
            pl.BlockSpec((seq, hg * HEAD_DIM), lambda b, g, i: (b, 2 * ng + g)),
            pl.BlockSpec((tq, LANES), lambda b, g, i: (b * nq + i, 0)),
            pl.BlockSpec((seq, LANES), lambda b, g, i: (b, 0), pipeline_mode=pl.Buffered(1)),
        ],
        out_specs=pl.BlockSpec((tq, hg * HEAD_DIM), lambda b, g, i: (b * nq + i, g)),
        out_shape=jax.ShapeDtypeStruct((m, heads * HEAD_DIM), BF16),
        scratch_shapes=[
            pltpu.VMEM((tq, hg * AUG_DIM), BF16),
            pltpu.VMEM((seq, hg * AUG_DIM), BF16),
            pltpu.VMEM((hg, tq, AUG_DIM), F32),
            pltpu.VMEM((hg, tq, LANES), F32),
            pltpu.VMEM((hg, tq, tk), F32),
        ],
        compiler_params=_params("parallel", "parallel", "arbitrary"),
        name="fox_attn",
    )(proj, proj, proj, decay, decay)


def _spatial_gating(u_ref, v_ref, g_ref, w_ref, bt_ref, *, groups):
    ts = u_ref.shape[0]
    vn = _rms_scale(_gelu(v_ref[...].astype(F32)), g_ref[...]).astype(BF16)
    r = lax.broadcasted_iota(jnp.int32, (SGU_CHUNK, SGU_CHUNK), 0)
    c = lax.broadcasted_iota(jnp.int32, (SGU_CHUNK, SGU_CHUNK), 1)
    out = []
    for g in range(groups):
        w = jnp.where(r >= c, w_ref[g], 0.0).astype(BF16)
        bias = bt_ref[:, g:g + 1]
        cols = slice(g * SGU_GROUP_DIM, (g + 1) * SGU_GROUP_DIM)
        chunks = []
        for ch in range(ts // SGU_CHUNK):
            rows = slice(ch * SGU_CHUNK, (ch + 1) * SGU_CHUNK)
            mixed = _dot(w, vn[rows, cols]) + bias
            chunks.append((_gelu(u_ref[rows, cols].astype(F32)) * mixed).astype(BF16))
        out.append(jnp.concatenate(chunks, axis=0))
    return jnp.concatenate(out, axis=1)


def _merge_kernel(ya_ref, u_ref, v_ref, gs_ref, ws_ref, bt_ref, wa_ref, wb_ref, ga0_ref, ga1_ref,
                  gb0_ref, gb1_ref, o_ref, *, groups):
    a = _dot(ya_ref[...], wa_ref[...])
    y_b = _spatial_gating(u_ref, v_ref, gs_ref, ws_ref, bt_ref, groups=groups)
    gate_a = jnp.concatenate([ga0_ref[...], ga1_ref[...]], axis=1).astype(F32)
    gate_b = jnp.concatenate([gb0_ref[...], gb1_ref[...]], axis=1).astype(F32)
    b = _sigmoid(gate_b) * _dot(y_b, wb_ref[...])
    o_ref[...] = (_sigmoid(gate_a) * a + b).astype(o_ref.dtype)


def _merge(y_a, proj, gain, w_s, b_s_t, w_a, w_b, layer, *, u_col, gate_col, tm=512):
    m, ka = y_a.shape
    groups = w_s.shape[1]
    kb = groups * SGU_GROUP_DIM
    n = w_a.shape[2]
    half = n // 2
    g0 = gate_col // half
    return pl.pallas_call(
        functools.partial(_merge_kernel, groups=groups),
        grid=(m // tm,),
        in_specs=[
            pl.BlockSpec((tm, ka), lambda i: (i, 0)),
            pl.BlockSpec((tm, kb), lambda i: (i, u_col)),
            pl.BlockSpec((tm, kb), lambda i: (i, u_col + 1)),
            pl.BlockSpec((None, 1, kb), lambda i: (layer, 0, 0)),
            pl.BlockSpec((None, groups, SGU_CHUNK, SGU_CHUNK), lambda i: (layer, 0, 0, 0)),
            pl.BlockSpec((None, SGU_CHUNK, groups), lambda i: (layer, 0, 0)),
            pl.BlockSpec((None, ka, n), lambda i: (layer, 0, 0)),
            pl.BlockSpec((None, kb, n), lambda i: (layer, 0, 0)),
            pl.BlockSpec((tm, half), lambda i: (i, g0)),
            pl.BlockSpec((tm, half), lambda i: (i, g0 + 1)),
            pl.BlockSpec((tm, half), lambda i: (i, g0 + 2)),
            pl.BlockSpec((tm, half), lambda i: (i, g0 + 3)),
        ],
        out_specs=pl.BlockSpec((tm, n), lambda i: (i, 0)),
        out_shape=jax.ShapeDtypeStruct((m, n), BF16),
        compiler_params=_params("parallel"),
        name="merge",
    )(y_a, proj, proj, gain, w_s, b_s_t, w_a, w_b, proj, proj, proj, proj)


def _outproj_kernel(a_ref, w_ref, x_ref, g_ref, xo_ref, h_ref):
    x_new = x_ref[...] + _dot(a_ref[...], w_ref[...])
    xo_ref[...] = x_new
    h_ref[...] = _rms_scale(x_new, g_ref[...]).astype(h_ref.dtype)


def _out_proj(a, w, x, gain, layer, *, tm=512):
    m, k = a.shape
    n = w.shape[2]
    return pl.pallas_call(
        _outproj_kernel,
        grid=(m // tm,),
        in_specs=[
            pl.BlockSpec((tm, k), lambda i: (i, 0)),
            pl.BlockSpec((None, k, n), lambda i: (layer, 0, 0)),
            pl.BlockSpec((tm, n), lambda i: (i, 0)),
            pl.BlockSpec((None, 1, n), lambda i: (layer, 0, 0)),
        ],
        out_specs=[pl.BlockSpec((tm, n), lambda i: (i, 0)), pl.BlockSpec((tm, n), lambda i: (i, 0))],
        out_shape=[jax.ShapeDtypeStruct((m, n), F32), jax.ShapeDtypeStruct((m, n), BF16)],
        compiler_params=_params("parallel"),
        name="out_proj",
    )(a, w, x, gain)


def _ffn_up_kernel(h_ref, wa_ref, wb_ref, cw_ref, cb_ref, o_ref, wa_s, wb_s, a_ref, tail_ref, *, tiles_per_seq):
    tm = h_ref.shape[0]
    i = pl.program_id(1)

    @pl.when(i == 0)
    def _():
        wa_s[...] = wa_ref[...].astype(BF16)
        wb_s[...] = wb_ref[...].astype(BF16)

    seq_start = i % tiles_per_seq == 0

    @pl.when(seq_start)
    def _():
        a_ref[0:HALO_ROWS, :] = jnp.zeros((HALO_ROWS, a_ref.shape[1]), F32)

    @pl.when(jnp.logical_not(seq_start))
    def _():
        a_ref[0:HALO_ROWS, :] = tail_ref[...]

    h = h_ref[...]
    a = _dot(h, wa_s[...])
    a_ref[HALO_ROWS:, :] = a
    conv = (cb_ref[...] + cw_ref[0:1, :] * a_ref[pl.ds(HALO_ROWS - 2, tm), :]
            + cw_ref[1:2, :] * a_ref[pl.ds(HALO_ROWS - 1, tm), :] + cw_ref[2:3, :] * a)
    o_ref[...] = (_gelu(conv) * _dot(h, wb_s[...])).astype(o_ref.dtype)
    tail_ref[...] = a_ref[tm:, :]


def _ffn_up(h, w_up, conv_w, conv_b, layer, *, seq, tm=1024, tn=512):
    m, d = h.shape
    dff = w_up.shape[2] // 2
    nb = dff // tn
    return pl.pallas_call(
        functools.partial(_ffn_up_kernel, tiles_per_seq=seq // tm),
        grid=(nb, m // tm),
        in_specs=[
            pl.BlockSpec((tm, d), lambda j, i: (i, 0)),
            pl.BlockSpec((None, d, tn), lambda j, i: (layer, 0, j)),
            pl.BlockSpec((None, d, tn), lambda j, i: (layer, 0, nb + j)),
            pl.BlockSpec((None, CONV_WIDTH, tn), lambda j, i: (layer, 0, j)),
            pl.BlockSpec((None, 1, tn), lambda j, i: (layer, 0, j)),
        ],
        out_specs=pl.BlockSpec((tm, tn), lambda j, i: (i, j)),
        out_shape=jax.ShapeDtypeStruct((m, dff), BF16),
        scratch_shapes=[
            pltpu.VMEM((d, tn), BF16),
            pltpu.VMEM((d, tn), BF16),
            pltpu.VMEM((HALO_ROWS + tm, tn), F32),
            pltpu.VMEM((HALO_ROWS, tn), F32),
        ],
        compiler_params=_params("parallel", "arbitrary"),
        name="ffn_up",
    )(h, w_up, w_up, conv_w, conv_b)


def _ffn_down_kernel(a_ref, w_ref, x_ref, o_ref):
    o_ref[...] = x_ref[...] + _dot(a_ref[...], w_ref[...])


def _ffn_down(a, w, x, layer, *, tm=1024, tn=512):
    m, k = a.shape
    n = w.shape[2]
    return pl.pallas_call(
        _ffn_down_kernel,
        grid=(m // tm, n // tn),
        in_specs=[
            pl.BlockSpec((tm, k), lambda i, j: (i, 0)),
            pl.BlockSpec((None, k, tn), lambda i, j: (layer, 0, j)),
            pl.BlockSpec((tm, tn), lambda i, j: (i, j)),
        ],
        out_specs=pl.BlockSpec((tm, tn), lambda i, j: (i, j)),
        out_shape=jax.ShapeDtypeStruct((m, n), F32),
        compiler_params=_params("parallel", "parallel"),
        name="ffn_down",
    )(a, w, x)


def _norm_kernel(x_ref, g_ref, o_ref):
    o_ref[...] = _rms_scale(x_ref[...], g_ref[...])


def _final_norm(x, gain, *, tm=256):
    m, d = x.shape
    return pl.pallas_call(
        _norm_kernel,
        grid=(m // tm,),
        in_specs=[pl.BlockSpec((tm, d), lambda i: (i, 0)), pl.BlockSpec((1, d), lambda i: (0, 0))],
        out_specs=pl.BlockSpec((tm, d), lambda i: (i, 0)),
        out_shape=jax.ShapeDtypeStruct((m, d), F32),
        compiler_params=_params("parallel"),
        name="final_norm",
    )(x, gain)


def kernel(x, g_mix, w_in, b_forget, g_sgu, w_spatial, b_spatial, w_branch_a, w_branch_b, w_out, g_ffn,
           w_up, conv_w, conv_b, w_down, g_final):
    batch, seq, d = x.shape
    depth = w_in.shape[0]
    heads = b_forget.shape[1]
    fox = heads * HEAD_DIM
    sgu_w = g_sgu.shape[1]
    f0 = 3 * fox
    u0 = 3 * fox
    gate0 = u0 + 2 * sgu_w
    assert fox == sgu_w, "column-block indexing of proj assumes equal branch widths"

    w_main, w_f = _stage_w_in(jnp.swapaxes(w_in, 1, 2), f0=f0, gap=heads)
    b_f = jnp.pad(b_forget, ((0, 0), (0, LANES - heads))).reshape(depth, 1, LANES)
    w_a_b, w_b_b, w_out_b, w_down_b = (w.astype(BF16) for w in (w_branch_a, w_branch_b, w_out, w_down))
    b_s_t = jnp.swapaxes(b_spatial, 1, 2)
    g_mix3, g_sgu3, g_ffn3 = (g.reshape(depth, 1, -1) for g in (g_mix, g_sgu, g_ffn))
    conv_b3 = conv_b.reshape(depth, 1, -1)

    xs = x.reshape(batch * seq, d)
    for l in range(depth):
        proj, logf = _in_proj(xs, g_mix3, w_main, w_f, b_f, l)
        decay = _fox_decay(logf, batch=batch, heads=heads)
        y_a = _fox_attn(proj, decay, batch=batch, heads=heads)
        merged = _merge(y_a, proj, g_sgu3, w_spatial, b_s_t, w_a_b, w_b_b, l, u_col=u0 // sgu_w, gate_col=gate0)
        xs, h_ffn = _out_proj(merged, w_out_b, xs, g_ffn3, l)
        g = _ffn_up(h_ffn, w_up, conv_w, conv_b3, l, seq=seq)
        xs = _ffn_down(g, w_down_b, xs, l)
    return _final_norm(xs, g_final.reshape(1, d)).reshape(batch, seq, d)
```

```python
import functools
import math

import jax
import jax.numpy as jnp
from jax import lax
from jax.experimental import pallas as pl
from jax.experimental.pallas import tpu as pltpu

F32 = jnp.float32
BF16 = jnp.bfloat16

HEAD_DIM = 128
SGU_GROUP_DIM = 128
SGU_CHUNK = 128
CONV_WIDTH = 3
RMS_EPS = 1e-6
LOG2E = math.log2(math.e)

LANES = 128
AUG_DIM = 2 * HEAD_DIM
HALO_ROWS = 8
VMEM_LIMIT = 56 * 1024 * 1024


def _dot(a, b):
    return jnp.dot(a, b, preferred_element_type=F32)


def _gelu(x):
    a = -2.0 * LOG2E * math.sqrt(2.0 / math.pi)
    return x / (1.0 + jnp.exp2(x * (a + (a * 0.044715) * (x * x))))


def _sigmoid(x):
    return 1.0 / (1.0 + jnp.exp2(-LOG2E * x))


def _rms_scale(x, gain):
    inv = lax.rsqrt(jnp.mean(x * x, axis=-1, keepdims=True) + RMS_EPS)
    return (x * inv) * gain


def _split3(x):
    hi = x.astype(BF16)
    r = x - hi.astype(F32)
    mid = r.astype(BF16)
    lo = (r - mid.astype(F32)).astype(BF16)
    return hi, mid, lo


def _params(*sem):
    return pltpu.CompilerParams(dimension_semantics=sem, vmem_limit_bytes=VMEM_LIMIT)


def _stage_kernel(w_ref, nxt_ref, o_ref, wf_ref, *, first_shifted, gap):
    t = pl.program_id(1)

    @pl.when(t < first_shifted)
    def _():
        o_ref[...] = w_ref[...].T.astype(BF16)

    @pl.when(t >= first_shifted)
    def _():
        o_ref[...] = jnp.concatenate([w_ref[gap:, :], nxt_ref[...]], axis=0).T.astype(BF16)

    @pl.when(t == first_shifted)
    def _():
        row = lax.broadcasted_iota(jnp.int32, (LANES, w_ref.shape[1]), 0)
        w_f = jnp.where(row < gap, w_ref[:LANES, :], 0.0).T
        hi = w_f.astype(BF16)
        wf_ref[:, :LANES] = hi
        wf_ref[:, LANES:] = (w_f - hi.astype(F32)).astype(BF16)


def _stage_w_in(w_in_t, *, f0, gap, tn=1024):
    depth, n_in, d = w_in_t.shape
    n = n_in - gap
    first_shifted = f0 // tn
    assert gap == 8, "the column gap must be one f32 sublane tile"
    return pl.pallas_call(
        functools.partial(_stage_kernel, first_shifted=first_shifted, gap=gap),
        grid=(depth, n // tn),
        in_specs=[
            pl.BlockSpec((None, tn, d), lambda l, t: (l, t, 0)),
            pl.BlockSpec((None, gap, d), lambda l, t: (l, (t + 1) * (tn // gap), 0)),
        ],
        out_specs=[
            pl.BlockSpec((None, d, tn), lambda l, t: (l, 0, t)),
            pl.BlockSpec((None, d, 2 * LANES), lambda l, t: (l, 0, 0)),
        ],
        out_shape=[
            jax.ShapeDtypeStruct((depth, d, n), BF16),
            jax.ShapeDtypeStruct((depth, d, 2 * LANES), BF16),
        ],
        compiler_params=_params("parallel", "arbitrary"),
        name="stage_w_in",
    )(w_in_t, w_in_t)


def _inproj_kernel(x_ref, g_ref, w_ref, wf_ref, bf_ref, o_ref, lf_ref, h_ref, *, rows):
    j = pl.program_id(1)

    @pl.when(j == 0)
    def _():
        for r0 in range(0, x_ref.shape[0], rows):
            sl = slice(r0, r0 + rows)
            hb = _rms_scale(x_ref[sl, :], g_ref[...]).astype(BF16)
            h_ref[sl, :] = hb
            zz = _dot(hb, wf_ref[...])
            z = (zz[:, :LANES] + zz[:, LANES:]) + bf_ref[...]
            lf_ref[sl, :] = jnp.minimum(z, 0.0) - jnp.log(1.0 + jnp.exp(-jnp.abs(z)))
            o_ref[sl, :] = _dot(hb, w_ref[...]).astype(o_ref.dtype)

    @pl.when(j > 0)
    def _():
        o_ref[...] = _dot(h_ref[...], w_ref[...]).astype(o_ref.dtype)


def _in_proj(x, gain, w, w_f, b_f, layer, *, tm=1024, tn=1536, rows=256):
    m, d = x.shape
    n = w.shape[2]
    return pl.pallas_call(
        functools.partial(_inproj_kernel, rows=rows),
        grid=(m // tm, n // tn),
        in_specs=[
            pl.BlockSpec((tm, d), lambda i, j: (i, 0)),
            pl.BlockSpec((None, 1, d), lambda i, j: (layer, 0, 0)),
            pl.BlockSpec((None, d, tn), lambda i, j: (layer, 0, j)),
            pl.BlockSpec((None, d, 2 * LANES), lambda i, j: (layer, 0, 0)),
            pl.BlockSpec((None, 1, LANES), lambda i, j: (layer, 0, 0)),
        ],
        out_specs=[
            pl.BlockSpec((tm, tn), lambda i, j: (i, j)),
            pl.BlockSpec((tm, LANES), lambda i, j: (i, 0)),
        ],
        out_shape=[
            jax.ShapeDtypeStruct((m, n), BF16),
            jax.ShapeDtypeStruct((m, LANES), F32),
        ],
        scratch_shapes=[pltpu.VMEM((tm, d), BF16)],
        compiler_params=_params("parallel", "arbitrary"),
        name="in_proj",
    )(x, gain, w, w_f, b_f)


def _decay_kernel(lf_ref, o_ref, carry_ref, *, heads):
    ts = lf_ref.shape[0]

    @pl.when(pl.program_id(1) == 0)
    def _():
        carry_ref[...] = jnp.zeros_like(carry_ref)

    r = lax.broadcasted_iota(jnp.int32, (ts, ts), 0)
    c = lax.broadcasted_iota(jnp.int32, (ts, ts), 1)
    tri = jnp.where(r >= c, 1.0, 0.0).astype(BF16)
    x1, x2, x3 = _split3(lf_ref[...])
    cum = (_dot(tri, x1) + _dot(tri, x2)) + _dot(tri, x3) + carry_ref[0:1, :]
    carry_ref[0:1, :] = cum[ts - 1:ts, :]

    lane = lax.broadcasted_iota(jnp.int32, cum.shape, 1)
    hi, mid, lo = _split3(jnp.where(lane < heads, cum * LOG2E, 0.0))
    packed = (hi.astype(F32) + pltpu.roll(mid.astype(F32), heads, 1)
              + pltpu.roll(lo.astype(F32), 2 * heads, 1))
    o_ref[...] = packed.astype(BF16)


def _fox_decay(logf, *, batch, heads, ts=512):
    m = logf.shape[0]
    nblk = m // batch // ts
    assert 3 * heads <= LANES
    return pl.pallas_call(
        functools.partial(_decay_kernel, heads=heads),
        grid=(batch, nblk),
        in_specs=[pl.BlockSpec((ts, LANES), lambda b, s: (b * nblk + s, 0))],
        out_specs=pl.BlockSpec((ts, LANES), lambda b, s: (b * nblk + s, 0)),
        out_shape=jax.ShapeDtypeStruct((m, LANES), BF16),
        scratch_shapes=[pltpu.VMEM((8, LANES), F32)],
        compiler_params=_params("parallel", "arbitrary"),
        name="fox_decay",
    )(logf)


def _decay_columns(parts, head0, hg, heads, lane0, sign):
    row = lax.broadcasted_iota(jnp.int32, (LANES, hg * LANES), 0)
    col = lax.broadcasted_iota(jnp.int32, (LANES, hg * LANES), 1)
    term, head = row // heads, row % heads
    hit = (term < 3) & (head == head0 + col // LANES) & (col % LANES == term + lane0)
    return _dot(parts, jnp.where(hit, sign, 0.0).astype(BF16))


def _attn_kernel(q_ref, k_ref, v_ref, dq_ref, dk_ref, o_ref, qa_ref, ka_ref, acc_ref, m_ref, s_ref,
                 *, hg, heads, tq, tk, q_scale, rows):
    i = pl.program_id(2)
    seq = k_ref.shape[0]
    head0 = pl.program_id(1) * hg
    lane_q = lax.broadcasted_iota(jnp.int32, (tq, LANES), 1)
    lane_k = lax.broadcasted_iota(jnp.int32, (rows, LANES), 1)

    @pl.when(i == 0)
    def _():
        def chunk(c, carry):
            sl = pl.ds(pl.multiple_of(c * rows, rows), rows)
            dk = _decay_columns(dk_ref[sl, :], head0, hg, heads, 3, -1.0)
            for h in range(hg):
                ext = jnp.where(lane_k < 3, 1.0, dk[:, h * LANES:(h + 1) * LANES])
                ka_ref[sl, h * AUG_DIM:h * AUG_DIM + HEAD_DIM] = k_ref[sl, h * HEAD_DIM:(h + 1) * HEAD_DIM]
                ka_ref[sl, h * AUG_DIM + HEAD_DIM:(h + 1) * AUG_DIM] = ext.astype(BF16)
            return carry
        lax.fori_loop(0, seq // rows, chunk, 0)

    dq = _decay_columns(dq_ref[...], head0, hg, heads, 0, 1.0)
    for h in range(hg):
        ext = jnp.where((lane_q >= 3) & (lane_q < 6), 1.0, dq[:, h * LANES:(h + 1) * LANES])
        qh = q_ref[:, h * HEAD_DIM:(h + 1) * HEAD_DIM].astype(F32) * q_scale
        qa_ref[:, h * AUG_DIM:h * AUG_DIM + HEAD_DIM] = qh.astype(BF16)
        qa_ref[:, h * AUG_DIM + HEAD_DIM:(h + 1) * AUG_DIM] = ext.astype(BF16)

    lane = lax.broadcasted_iota(jnp.int32, (tk, HEAD_DIM), 1)
    ones_col = jnp.where(lane == 0, 1.0, 0.0).astype(BF16)
    causal_bias = jnp.where(lax.broadcasted_iota(jnp.int32, (tk, tk), 1)
                            <= lax.broadcasted_iota(jnp.int32, (tk, tk), 0), 0.0, -1e30)

    def scores(h, j0, r0):
        q = qa_ref[r0:, h * AUG_DIM:(h + 1) * AUG_DIM]
        k = ka_ref[pl.ds(j0, tk), h * AUG_DIM:(h + 1) * AUG_DIM]
        return lax.dot_general(q, k, (((1,), (1,)), ((), ())), preferred_element_type=F32)

    def fold(h, s, j0, r0, masked):
        v = v_ref[pl.ds(j0, tk), h * HEAD_DIM:(h + 1) * HEAD_DIM]
        if masked:
            top = s[:tk] + causal_bias
            s = top if s.shape[0] == tk else jnp.concatenate([top, s[tk:]], axis=0)
        m_prev = m_ref[h, r0:, :]
        m_new = jnp.maximum(m_prev, jnp.max(s, axis=-1, keepdims=True))
        m_ref[h, r0:, :] = m_new
        p = jnp.exp2(s - jnp.tile(m_new, (1, tk // LANES))).astype(BF16)
        pv = _dot(p, jnp.concatenate([v, ones_col], axis=-1))
        alpha = jnp.tile(jnp.exp2(m_prev - m_new), (1, AUG_DIM // LANES))
        acc_ref[h, r0:, :] = acc_ref[h, r0:, :] * alpha + pv

    for h in range(hg):
        s_ref[h] = scores(h, 0, 0)
    m_ref[...] = jnp.full(m_ref.shape, -1e30, F32)
    acc_ref[...] = jnp.zeros(acc_ref.shape, F32)

    def body(j, carry):
        j0 = pl.multiple_of(j * tk, tk)
        for h in range(hg):
            s = s_ref[h]
            s_ref[h] = scores(h, j0 + tk, 0)
            fold(h, s, j0, 0, False)
        return carry

    lax.fori_loop(0, i * (tq // tk), body, 0)
    nd = tq // tk
    for d in range(nd):
        j0 = pl.multiple_of(i * tq + d * tk, tk)
        r0 = d * tk
        for h in range(hg):
            s = s_ref[h, r0:, :]
            if d + 1 < nd:
                s_ref[h, r0 + tk:, :] = scores(h, j0 + tk, r0 + tk)
            fold(h, s, j0, r0, True)
    for h in range(hg):
        acc = acc_ref[h]
        out = acc[:, :HEAD_DIM] / acc[:, HEAD_DIM:HEAD_DIM + 1]
        o_ref[:, h * HEAD_DIM:(h + 1) * HEAD_DIM] = out.astype(o_ref.dtype)


def _fox_attn(proj, decay, *, batch, heads, hg=4, tq=1024, tk=512, rows=512):
    m = proj.shape[0]
    seq = m // batch
    nq = seq // tq
    ng = heads // hg
    return pl.pallas_call(
        functools.partial(_attn_kernel, hg=hg, heads=heads, tq=tq, tk=tk,
                          q_scale=HEAD_DIM ** -0.5 * LOG2E, rows=rows),
        grid=(batch, ng, nq),
        in_specs=[
            pl.BlockSpec((tq, hg * HEAD_DIM), lambda b, g, i: (b * nq + i, g)),
            pl.BlockSpec((seq, hg * HEAD_DIM), lambda b, g, i: (b, ng + g), pipeline_mode=pl.Buffered(1)),
            pl.BlockSpec((seq, hg * HEAD_DIM), lambda b, g, i: (b, 2 * ng + g)),
            pl.BlockSpec((tq, LANES), lambda b, g, i: (b * nq + i, 0)),
            pl.BlockSpec((seq, LANES), lambda b, g, i: (b, 0), pipeline_mode=pl.Buffered(1)),
        ],
        out_specs=pl.BlockSpec((tq, hg * HEAD_DIM), lambda b, g, i: (b * nq + i, g)),
        out_shape=jax.ShapeDtypeStruct((m, heads * HEAD_DIM), BF16),
        scratch_shapes=[
            pltpu.VMEM((tq, hg * AUG_DIM), BF16),
            pltpu.VMEM((seq, hg * AUG_DIM), BF16),
            pltpu.VMEM((hg, tq, AUG_DIM), F32),
            pltpu.VMEM((hg, tq, LANES), F32),
            pltpu.VMEM((hg, tq, tk), F32),
        ],
        compiler_params=_params("parallel", "parallel", "arbitrary"),
        name="fox_attn",
    )(proj, proj, proj, decay, decay)


def _spatial_gating(u_ref, v_ref, g_ref, w_ref, bt_ref, *, groups):
    ts = u_ref.shape[0]
    vn = _rms_scale(_gelu(v_ref[...].astype(F32)), g_ref[...]).astype(BF16)
    r = lax.broadcasted_iota(jnp.int32, (SGU_CHUNK, SGU_CHUNK), 0)
    c = lax.broadcasted_iota(jnp.int32, (SGU_CHUNK, SGU_CHUNK), 1)
    out = []
    for g in range(groups):
        w = jnp.where(r >= c, w_ref[g], 0.0).astype(BF16)
        bias = bt_ref[:, g:g + 1]
        cols = slice(g * SGU_GROUP_DIM, (g + 1) * SGU_GROUP_DIM)
        chunks = []
        for ch in range(ts // SGU_CHUNK):
            rows = slice(ch * SGU_CHUNK, (ch + 1) * SGU_CHUNK)
            mixed = _dot(w, vn[rows, cols]) + bias
            chunks.append((_gelu(u_ref[rows, cols].astype(F32)) * mixed).astype(BF16))
        out.append(jnp.concatenate(chunks, axis=0))
    return jnp.concatenate(out, axis=1)


def _merge_kernel(ya_ref, u_ref, v_ref, gs_ref, ws_ref, bt_ref, wa_ref, wb_ref, ga0_ref, ga1_ref,
                  gb0_ref, gb1_ref, o_ref, wa_s, wb_s, *, groups):
    @pl.when(pl.program_id(0) == 0)
    def _():
        wa_s[...] = wa_ref[...].astype(BF16)
        wb_s[...] = wb_ref[...].astype(BF16)

    a = _dot(ya_ref[...], wa_s[...])
    y_b = _spatial_gating(u_ref, v_ref, gs_ref, ws_ref, bt_ref, groups=groups)
    gate_a = jnp.concatenate([ga0_ref[...], ga1_ref[...]], axis=1).astype(F32)
    gate_b = jnp.concatenate([gb0_ref[...], gb1_ref[...]], axis=1).astype(F32)
    b = _sigmoid(gate_b) * _dot(y_b, wb_s[...])
    o_ref[...] = (_sigmoid(gate_a) * a + b).astype(o_ref.dtype)


def _merge(y_a, proj, gain, w_s, b_s_t, w_a, w_b, layer, *, u_col, gate_col, tm=512):
    m, ka = y_a.shape
    groups = w_s.shape[1]
    kb = groups * SGU_GROUP_DIM
    n = w_a.shape[2]
    half = n // 2
    g0 = gate_col // half
    return pl.pallas_call(
        functools.partial(_merge_kernel, groups=groups),
        grid=(m // tm,),
        in_specs=[
            pl.BlockSpec((tm, ka), lambda i: (i, 0)),
            pl.BlockSpec((tm, kb), lambda i: (i, u_col)),
            pl.BlockSpec((tm, kb), lambda i: (i, u_col + 1)),
            pl.BlockSpec((None, 1, kb), lambda i: (layer, 0, 0)),
            pl.BlockSpec((None, groups, SGU_CHUNK, SGU_CHUNK), lambda i: (layer, 0, 0, 0)),
            pl.BlockSpec((None, SGU_CHUNK, groups), lambda i: (layer, 0, 0)),
            pl.BlockSpec((None, ka, n), lambda i: (layer, 0, 0), pipeline_mode=pl.Buffered(1)),
            pl.BlockSpec((None, kb, n), lambda i: (layer, 0, 0), pipeline_mode=pl.Buffered(1)),
            pl.BlockSpec((tm, half), lambda i: (i, g0)),
            pl.BlockSpec((tm, half), lambda i: (i, g0 + 1)),
            pl.BlockSpec((tm, half), lambda i: (i, g0 + 2)),
            pl.BlockSpec((tm, half), lambda i: (i, g0 + 3)),
        ],
        out_specs=pl.BlockSpec((tm, n), lambda i: (i, 0)),
        out_shape=jax.ShapeDtypeStruct((m, n), BF16),
        scratch_shapes=[pltpu.VMEM((ka, n), BF16), pltpu.VMEM((kb, n), BF16)],
        compiler_params=_params("arbitrary"),
        name="merge",
    )(y_a, proj, proj, gain, w_s, b_s_t, w_a, w_b, proj, proj, proj, proj)


def _outproj_kernel(a_ref, w_ref, x_ref, g_ref, xo_ref, h_ref, w_s):
    @pl.when(pl.program_id(0) == 0)
    def _():
        w_s[...] = w_ref[...].astype(BF16)

    x_new = x_ref[...] + _dot(a_ref[...], w_s[...])
    xo_ref[...] = x_new
    h_ref[...] = _rms_scale(x_new, g_ref[...]).astype(h_ref.dtype)


def _out_proj(a, w, x, gain, layer, *, tm=512):
    m, k = a.shape
    n = w.shape[2]
    return pl.pallas_call(
        _outproj_kernel,
        grid=(m // tm,),
        in_specs=[
            pl.BlockSpec((tm, k), lambda i: (i, 0)),
            pl.BlockSpec((None, k, n), lambda i: (layer, 0, 0), pipeline_mode=pl.Buffered(1)),
            pl.BlockSpec((tm, n), lambda i: (i, 0)),
            pl.BlockSpec((None, 1, n), lambda i: (layer, 0, 0)),
        ],
        out_specs=[pl.BlockSpec((tm, n), lambda i: (i, 0)), pl.BlockSpec((tm, n), lambda i: (i, 0))],
        out_shape=[jax.ShapeDtypeStruct((m, n), F32), jax.ShapeDtypeStruct((m, n), BF16)],
        scratch_shapes=[pltpu.VMEM((k, n), BF16)],
        compiler_params=_params("arbitrary"),
        name="out_proj",
    )(a, w, x, gain)


def _ffn_up_kernel(h_ref, wa_ref, wb_ref, cw_ref, cb_ref, o_ref, wa_s, wb_s, a_ref, tail_ref, *, tiles_per_seq):
    tm = h_ref.shape[0]
    i = pl.program_id(1)

    @pl.when(i == 0)
    def _():
        wa_s[...] = wa_ref[...].astype(BF16)
        wb_s[...] = wb_ref[...].astype(BF16)

    seq_start = i % tiles_per_seq == 0

    @pl.when(seq_start)
    def _():
        a_ref[0:HALO_ROWS, :] = jnp.zeros((HALO_ROWS, a_ref.shape[1]), F32)

    @pl.when(jnp.logical_not(seq_start))
    def _():
        a_ref[0:HALO_ROWS, :] = tail_ref[...]

    h = h_ref[...]
    a = _dot(h, wa_s[...])
    a_ref[HALO_ROWS:, :] = a
    conv = (cb_ref[...] + cw_ref[0:1, :] * a_ref[pl.ds(HALO_ROWS - 2, tm), :]
            + cw_ref[1:2, :] * a_ref[pl.ds(HALO_ROWS - 1, tm), :] + cw_ref[2:3, :] * a)
    o_ref[...] = (_gelu(conv) * _dot(h, wb_s[...])).astype(o_ref.dtype)
    tail_ref[...] = a_ref[tm:, :]


def _ffn_up(h, w_up, conv_w, conv_b, layer, *, seq, tm=1024, tn=512):
    m, d = h.shape
    dff = w_up.shape[2] // 2
    nb = dff // tn
    return pl.pallas_call(
        functools.partial(_ffn_up_kernel, tiles_per_seq=seq // tm),
        grid=(nb, m // tm),
        in_specs=[
            pl.BlockSpec((tm, d), lambda j, i: (i, 0)),
            pl.BlockSpec((None, d, tn), lambda j, i: (layer, 0, j)),
            pl.BlockSpec((None, d, tn), lambda j, i: (layer, 0, nb + j)),
            pl.BlockSpec((None, CONV_WIDTH, tn), lambda j, i: (layer, 0, j)),
            pl.BlockSpec((None, 1, tn), lambda j, i: (layer, 0, j)),
        ],
        out_specs=pl.BlockSpec((tm, tn), lambda j, i: (i, j)),
        out_shape=jax.ShapeDtypeStruct((m, dff), BF16),
        scratch_shapes=[
            pltpu.VMEM((d, tn), BF16),
            pltpu.VMEM((d, tn), BF16),
            pltpu.VMEM((HALO_ROWS + tm, tn), F32),
            pltpu.VMEM((HALO_ROWS, tn), F32),
        ],
        compiler_params=_params("parallel", "arbitrary"),
        name="ffn_up",
    )(h, w_up, w_up, conv_w, conv_b)


def _ffn_down_kernel(a_ref, w_ref, x_ref, o_ref):
    o_ref[...] = x_ref[...] + _dot(a_ref[...], w_ref[...])


def _ffn_down(a, w, x, layer, *, tm=1024, tn=512):
    m, k = a.shape
    n = w.shape[2]
    return pl.pallas_call(
        _ffn_down_kernel,
        grid=(m // tm, n // tn),
        in_specs=[
            pl.BlockSpec((tm, k), lambda i, j: (i, 0)),
            pl.BlockSpec((None, k, tn), lambda i, j: (layer, 0, j)),
            pl.BlockSpec((tm, tn), lambda i, j: (i, j)),
        ],
        out_specs=pl.BlockSpec((tm, tn), lambda i, j: (i, j)),
        out_shape=jax.ShapeDtypeStruct((m, n), F32),
        compiler_params=_params("parallel", "parallel"),
        name="ffn_down",
    )(a, w, x)


def _norm_kernel(x_ref, g_ref, o_ref):
    o_ref[...] = _rms_scale(x_ref[...], g_ref[...])


def _final_norm(x, gain, *, tm=256):
    m, d = x.shape
    return pl.pallas_call(
        _norm_kernel,
        grid=(m // tm,),
        in_specs=[pl.BlockSpec((tm, d), lambda i: (i, 0)), pl.BlockSpec((1, d), lambda i: (0, 0))],
        out_specs=pl.BlockSpec((tm, d), lambda i: (i, 0)),
        out_shape=jax.ShapeDtypeStruct((m, d), F32),
        compiler_params=_params("parallel"),
        name="final_norm",
    )(x, gain)


def kernel(x, g_mix, w_in, b_forget, g_sgu, w_spatial, b_spatial, w_branch_a, w_branch_b, w_out, g_ffn,
           w_up, conv_w, conv_b, w_down, g_final):
    batch, seq, d = x.shape
    depth = w_in.shape[0]
    heads = b_forget.shape[1]
    fox = heads * HEAD_DIM
    sgu_w = g_sgu.shape[1]
    f0 = 3 * fox
    u0 = 3 * fox
    gate0 = u0 + 2 * sgu_w
    assert fox == sgu_w, "column-block indexing of proj assumes equal branch widths"

    w_main, w_f = _stage_w_in(jnp.swapaxes(w_in, 1, 2), f0=f0, gap=heads)
    b_f = jnp.pad(b_forget, ((0, 0), (0, LANES - heads))).reshape(depth, 1, LANES)
    w_down_b = w_down.astype(BF16)
    b_s_t = jnp.swapaxes(b_spatial, 1, 2)
    g_mix3, g_sgu3, g_ffn3 = (g.reshape(depth, 1, -1) for g in (g_mix, g_sgu, g_ffn))
    conv_b3 = conv_b.reshape(depth, 1, -1)

    xs = x.reshape(batch * seq, d)
    for l in range(depth):
        proj, logf = _in_proj(xs, g_mix3, w_main, w_f, b_f, l)
        decay = _fox_decay(logf, batch=batch, heads=heads)
        y_a = _fox_attn(proj, decay, batch=batch, heads=heads)
        merged = _merge(y_a, proj, g_sgu3, w_spatial, b_s_t, w_branch_a, w_branch_b, l, u_col=u0 // sgu_w,
                        gate_col=gate0)
        xs, h_ffn = _out_proj(merged, w_out, xs, g_ffn3, l)
        g = _ffn_up(h_ffn, w_up, conv_w, conv_b3, l, seq=seq)
        xs = _ffn_down(g, w_down_b, xs, l)
    return _final_norm(xs, g_final.reshape(1, d)).reshape(batch, seq, d)
```

```python
import functools
import math

import jax
import jax.numpy as jnp
from jax import lax
from jax.experimental import pallas as pl
from jax.experimental.pallas import tpu as pltpu

F32 = jnp.float32
BF16 = jnp.bfloat16

HEAD_DIM = 128
SGU_GROUP_DIM = 128
SGU_CHUNK = 128
CONV_WIDTH = 3
RMS_EPS = 1e-6
LOG2E = math.log2(math.e)

LANES = 128
AUG_DIM = 2 * HEAD_DIM
HALO_ROWS = 8
VMEM_LIMIT = 56 * 1024 * 1024


def _dot(a, b):
    return jnp.dot(a, b, preferred_element_type=F32)


def _gelu(x):
    a = -2.0 * LOG2E * math.sqrt(2.0 / math.pi)
    return x / (1.0 + jnp.exp2(x * (a + (a * 0.044715) * (x * x))))


def _sigmoid(x):
    return 1.0 / (1.0 + jnp.exp2(-LOG2E * x))


def _rms_scale(x, gain):
    inv = lax.rsqrt(jnp.mean(x * x, axis=-1, keepdims=True) + RMS_EPS)
    return (x * inv) * gain


def _split3(x):
    hi = x.astype(BF16)
    r = x - hi.astype(F32)
    mid = r.astype(BF16)
    lo = (r - mid.astype(F32)).astype(BF16)
    return hi, mid, lo


def _params(*sem):
    return pltpu.CompilerParams(dimension_semantics=sem, vmem_limit_bytes=VMEM_LIMIT)


def _stage_kernel(w_ref, nxt_ref, o_ref, wf_ref, *, first_shifted, gap):
    t = pl.program_id(1)

    @pl.when(t < first_shifted)
    def _():
        o_ref[...] = w_ref[...].T.astype(BF16)

    @pl.when(t >= first_shifted)
    def _():
        o_ref[...] = jnp.concatenate([w_ref[gap:, :], nxt_ref[...]], axis=0).T.astype(BF16)

    @pl.when(t == first_shifted)
    def _():
        row = lax.broadcasted_iota(jnp.int32, (LANES, w_ref.shape[1]), 0)
        w_f = jnp.where(row < gap, w_ref[:LANES, :], 0.0).T
        hi = w_f.astype(BF16)
        wf_ref[:, :LANES] = hi
        wf_ref[:, LANES:] = (w_f - hi.astype(F32)).astype(BF16)


def _stage_w_in(w_in_t, *, f0, gap, tn=1024):
    depth, n_in, d = w_in_t.shape
    n = n_in - gap
    first_shifted = f0 // tn
    assert gap == 8, "the column gap must be one f32 sublane tile"
    return pl.pallas_call(
        functools.partial(_stage_kernel, first_shifted=first_shifted, gap=gap),
        grid=(depth, n // tn),
        in_specs=[
            pl.BlockSpec((None, tn, d), lambda l, t: (l, t, 0)),
            pl.BlockSpec((None, gap, d), lambda l, t: (l, (t + 1) * (tn // gap), 0)),
        ],
        out_specs=[
            pl.BlockSpec((None, d, tn), lambda l, t: (l, 0, t)),
            pl.BlockSpec((None, d, 2 * LANES), lambda l, t: (l, 0, 0)),
        ],
        out_shape=[
            jax.ShapeDtypeStruct((depth, d, n), BF16),
            jax.ShapeDtypeStruct((depth, d, 2 * LANES), BF16),
        ],
        compiler_params=_params("parallel", "arbitrary"),
        name="stage_w_in",
    )(w_in_t, w_in_t)


def _inproj_kernel(x_ref, g_ref, w_ref, wf_ref, bf_ref, o_ref, lf_ref, h_ref, *, rows):
    j = pl.program_id(1)

    @pl.when(j == 0)
    def _():
        for r0 in range(0, x_ref.shape[0], rows):
            sl = slice(r0, r0 + rows)
            hb = _rms_scale(x_ref[sl, :], g_ref[...]).astype(BF16)
            h_ref[sl, :] = hb
            zz = _dot(hb, wf_ref[...])
            z = (zz[:, :LANES] + zz[:, LANES:]) + bf_ref[...]
            lf_ref[sl, :] = jnp.minimum(z, 0.0) - jnp.log(1.0 + jnp.exp(-jnp.abs(z)))
            o_ref[sl, :] = _dot(hb, w_ref[...]).astype(o_ref.dtype)

    @pl.when(j > 0)
    def _():
        o_ref[...] = _dot(h_ref[...], w_ref[...]).astype(o_ref.dtype)


def _in_proj(x, gain, w, w_f, b_f, layer, *, tm=1024, tn=1536, rows=256):
    m, d = x.shape
    n = w.shape[2]
    return pl.pallas_call(
        functools.partial(_inproj_kernel, rows=rows),
        grid=(m // tm, n // tn),
        in_specs=[
            pl.BlockSpec((tm, d), lambda i, j: (i, 0)),
            pl.BlockSpec((None, 1, d), lambda i, j: (layer, 0, 0)),
            pl.BlockSpec((None, d, tn), lambda i, j: (layer, 0, j)),
            pl.BlockSpec((None, d, 2 * LANES), lambda i, j: (layer, 0, 0)),
            pl.BlockSpec((None, 1, LANES), lambda i, j: (layer, 0, 0)),
        ],
        out_specs=[
            pl.BlockSpec((tm, tn), lambda i, j: (i, j)),
            pl.BlockSpec((tm, LANES), lambda i, j: (i, 0)),
        ],
        out_shape=[
            jax.ShapeDtypeStruct((m, n), BF16),
            jax.ShapeDtypeStruct((m, LANES), F32),
        ],
        scratch_shapes=[pltpu.VMEM((tm, d), BF16)],
        compiler_params=_params("parallel", "arbitrary"),
        name="in_proj",
    )(x, gain, w, w_f, b_f)


def _decay_kernel(lf_ref, o_ref, carry_ref, *, heads):
    ts = lf_ref.shape[0]

    @pl.when(pl.program_id(1) == 0)
    def _():
        carry_ref[...] = jnp.zeros_like(carry_ref)

    r = lax.broadcasted_iota(jnp.int32, (ts, ts), 0)
    c = lax.broadcasted_iota(jnp.int32, (ts, ts), 1)
    tri = jnp.where(r >= c, 1.0, 0.0).astype(BF16)
    x1, x2, x3 = _split3(lf_ref[...])
    cum = (_dot(tri, x1) + _dot(tri, x2)) + _dot(tri, x3) + carry_ref[0:1, :]
    carry_ref[0:1, :] = cum[ts - 1:ts, :]

    lane = lax.broadcasted_iota(jnp.int32, cum.shape, 1)
    hi, mid, lo = _split3(jnp.where(lane < heads, cum * LOG2E, 0.0))
    packed = (hi.astype(F32) + pltpu.roll(mid.astype(F32), heads, 1)
              + pltpu.roll(lo.astype(F32), 2 * heads, 1))
    o_ref[...] = packed.astype(BF16)


def _fox_decay(logf, *, batch, heads, ts=512):
    m = logf.shape[0]
    nblk = m // batch // ts
    assert 3 * heads <= LANES
    return pl.pallas_call(
        functools.partial(_decay_kernel, heads=heads),
        grid=(batch, nblk),
        in_specs=[pl.BlockSpec((ts, LANES), lambda b, s: (b * nblk + s, 0))],
        out_specs=pl.BlockSpec((ts, LANES), lambda b, s: (b * nblk + s, 0)),
        out_shape=jax.ShapeDtypeStruct((m, LANES), BF16),
        scratch_shapes=[pltpu.VMEM((8, LANES), F32)],
        compiler_params=_params("parallel", "arbitrary"),
        name="fox_decay",
    )(logf)


def _decay_columns(parts, head0, hg, heads, lane0, sign):
    row = lax.broadcasted_iota(jnp.int32, (LANES, hg * LANES), 0)
    col = lax.broadcasted_iota(jnp.int32, (LANES, hg * LANES), 1)
    term, head = row // heads, row % heads
    hit = (term < 3) & (head == head0 + col // LANES) & (col % LANES == term + lane0)
    return _dot(parts, jnp.where(hit, sign, 0.0).astype(BF16))


def _attn_kernel(q_ref, k_ref, v_ref, dq_ref, dk_ref, o_ref, qa_ref, ka_ref, acc_ref, m_ref, s_ref,
                 *, hg, heads, tq, tk, q_scale, rows):
    i = pl.program_id(2)
    seq = k_ref.shape[0]
    head0 = pl.program_id(1) * hg
    lane_q = lax.broadcasted_iota(jnp.int32, (tq, LANES), 1)
    lane_k = lax.broadcasted_iota(jnp.int32, (rows, LANES), 1)

    @pl.when(i == 0)
    def _():
        def chunk(c, carry):
            sl = pl.ds(pl.multiple_of(c * rows, rows), rows)
            dk = _decay_columns(dk_ref[sl, :], head0, hg, heads, 3, -1.0)
            for h in range(hg):
                ext = jnp.where(lane_k < 3, 1.0, dk[:, h * LANES:(h + 1) * LANES])
                ka_ref[sl, h * AUG_DIM:h * AUG_DIM + HEAD_DIM] = k_ref[sl, h * HEAD_DIM:(h + 1) * HEAD_DIM]
                ka_ref[sl, h * AUG_DIM + HEAD_DIM:(h + 1) * AUG_DIM] = ext.astype(BF16)
            return carry
        lax.fori_loop(0, seq // rows, chunk, 0)

    dq = _decay_columns(dq_ref[...], head0, hg, heads, 0, 1.0)
    for h in range(hg):
        ext = jnp.where((lane_q >= 3) & (lane_q < 6), 1.0, dq[:, h * LANES:(h + 1) * LANES])
        qh = q_ref[:, h * HEAD_DIM:(h + 1) * HEAD_DIM].astype(F32) * q_scale
        qa_ref[:, h * AUG_DIM:h * AUG_DIM + HEAD_DIM] = qh.astype(BF16)
        qa_ref[:, h * AUG_DIM + HEAD_DIM:(h + 1) * AUG_DIM] = ext.astype(BF16)

    lane = lax.broadcasted_iota(jnp.int32, (tk, HEAD_DIM), 1)
    ones_col = jnp.where(lane == 0, 1.0, 0.0).astype(BF16)
    causal_bias = jnp.where(lax.broadcasted_iota(jnp.int32, (tk, tk), 1)
                            <= lax.broadcasted_iota(jnp.int32, (tk, tk), 0), 0.0, -1e30)

    def scores(h, j0, r0):
        q = qa_ref[r0:, h * AUG_DIM:(h + 1) * AUG_DIM]
        k = ka_ref[pl.ds(j0, tk), h * AUG_DIM:(h + 1) * AUG_DIM]
        return lax.dot_general(q, k, (((1,), (1,)), ((), ())), preferred_element_type=F32)

    def fold(h, s, j0, r0, masked):
        v = v_ref[pl.ds(j0, tk), h * HEAD_DIM:(h + 1) * HEAD_DIM]
        if masked:
            top = s[:tk] + causal_bias
            s = top if s.shape[0] == tk else jnp.concatenate([top, s[tk:]], axis=0)
        m_prev = m_ref[h, r0:, :]
        m_new = jnp.maximum(m_prev, jnp.max(s, axis=-1, keepdims=True))
        m_ref[h, r0:, :] = m_new
        p = jnp.exp2(s - jnp.tile(m_new, (1, tk // LANES))).astype(BF16)
        pv = _dot(p, jnp.concatenate([v, ones_col], axis=-1))
        alpha = jnp.tile(jnp.exp2(m_prev - m_new), (1, AUG_DIM // LANES))
        acc_ref[h, r0:, :] = acc_ref[h, r0:, :] * alpha + pv

    for h in range(hg):
        s_ref[h] = scores(h, 0, 0)
    m_ref[...] = jnp.full(m_ref.shape, -1e30, F32)
    acc_ref[...] = jnp.zeros(acc_ref.shape, F32)

    def body(j, carry):
        j0 = pl.multiple_of(j * tk, tk)
        for h in range(hg):
            s = s_ref[h]
            s_ref[h] = scores(h, j0 + tk, 0)
            fold(h, s, j0, 0, False)
        return carry

    lax.fori_loop(0, i * (tq // tk), body, 0)
    nd = tq // tk
    for d in range(nd):
        j0 = pl.multiple_of(i * tq + d * tk, tk)
        r0 = d * tk
        for h in range(hg):
            s = s_ref[h, r0:, :]
            if d + 1 < nd:
                s_ref[h, r0 + tk:, :] = scores(h, j0 + tk, r0 + tk)
            fold(h, s, j0, r0, True)
    for h in range(hg):
        acc = acc_ref[h]
        out = acc[:, :HEAD_DIM] / acc[:, HEAD_DIM:HEAD_DIM + 1]
        o_ref[:, h * HEAD_DIM:(h + 1) * HEAD_DIM] = out.astype(o_ref.dtype)


def _fox_attn(proj, decay, *, batch, heads, hg=4, tq=1024, tk=512, rows=1024):
    m = proj.shape[0]
    seq = m // batch
    nq = seq // tq
    ng = heads // hg
    return pl.pallas_call(
        functools.partial(_attn_kernel, hg=hg, heads=heads, tq=tq, tk=tk,
                          q_scale=HEAD_DIM ** -0.5 * LOG2E, rows=rows),
        grid=(batch, ng, nq),
        in_specs=[
            pl.BlockSpec((tq, hg * HEAD_DIM), lambda b, g, i: (b * nq + i, g)),
            pl.BlockSpec((seq, hg * HEAD_DIM), lambda b, g, i: (b, ng + g), pipeline_mode=pl.Buffered(1)),
            pl.BlockSpec((seq, hg * HEAD_DIM), lambda b, g, i: (b, 2 * ng + g)),
            pl.BlockSpec((tq, LANES), lambda b, g, i: (b * nq + i, 0)),
            pl.BlockSpec((seq, LANES), lambda b, g, i: (b, 0), pipeline_mode=pl.Buffered(1)),
        ],
        out_specs=pl.BlockSpec((tq, hg * HEAD_DIM), lambda b, g, i: (b * nq + i, g)),
        out_shape=jax.ShapeDtypeStruct((m, heads * HEAD_DIM), BF16),
        scratch_shapes=[
            pltpu.VMEM((tq, hg * AUG_DIM), BF16),
            pltpu.VMEM((seq, hg * AUG_DIM), BF16),
            pltpu.VMEM((hg, tq, AUG_DIM), F32),
            pltpu.VMEM((hg, tq, LANES), F32),
            pltpu.VMEM((hg, tq, tk), F32),
        ],
        compiler_params=_params("parallel", "parallel", "arbitrary"),
        name="fox_attn",
    )(proj, proj, proj, decay, decay)


def _spatial_gating(u_ref, v_ref, g_ref, w_ref, bt_ref, *, groups):
    ts = u_ref.shape[0]
    vn = _rms_scale(_gelu(v_ref[...].astype(F32)), g_ref[...]).astype(BF16)
    r = lax.broadcasted_iota(jnp.int32, (SGU_CHUNK, SGU_CHUNK), 0)
    c = lax.broadcasted_iota(jnp.int32, (SGU_CHUNK, SGU_CHUNK), 1)
    out = []
    for g in range(groups):
        w = jnp.where(r >= c, w_ref[g], 0.0).astype(BF16)
        bias = bt_ref[:, g:g + 1]
        cols = slice(g * SGU_GROUP_DIM, (g + 1) * SGU_GROUP_DIM)
        chunks = []
        for ch in range(ts // SGU_CHUNK):
            rows = slice(ch * SGU_CHUNK, (ch + 1) * SGU_CHUNK)
            mixed = _dot(w, vn[rows, cols]) + bias
            chunks.append((_gelu(u_ref[rows, cols].astype(F32)) * mixed).astype(BF16))
        out.append(jnp.concatenate(chunks, axis=0))
    return jnp.concatenate(out, axis=1)


def _merge_kernel(ya_ref, u_ref, v_ref, gs_ref, ws_ref, bt_ref, wa_ref, wb_ref, ga0_ref, ga1_ref,
                  gb0_ref, gb1_ref, o_ref, wa_s, wb_s, *, groups):
    @pl.when(pl.program_id(0) == 0)
    def _():
        wa_s[...] = wa_ref[...].astype(BF16)
        wb_s[...] = wb_ref[...].astype(BF16)

    a = _dot(ya_ref[...], wa_s[...])
    y_b = _spatial_gating(u_ref, v_ref, gs_ref, ws_ref, bt_ref, groups=groups)
    gate_a = jnp.concatenate([ga0_ref[...], ga1_ref[...]], axis=1).astype(F32)
    gate_b = jnp.concatenate([gb0_ref[...], gb1_ref[...]], axis=1).astype(F32)
    b = _sigmoid(gate_b) * _dot(y_b, wb_s[...])
    o_ref[...] = (_sigmoid(gate_a) * a + b).astype(o_ref.dtype)


def _merge(y_a, proj, gain, w_s, b_s_t, w_a, w_b, layer, *, u_col, gate_col, tm=512):
    m, ka = y_a.shape
    groups = w_s.shape[1]
    kb = groups * SGU_GROUP_DIM
    n = w_a.shape[2]
    half = n // 2
    g0 = gate_col // half
    return pl.pallas_call(
        functools.partial(_merge_kernel, groups=groups),
        grid=(m // tm,),
        in_specs=[
            pl.BlockSpec((tm, ka), lambda i: (i, 0)),
            pl.BlockSpec((tm, kb), lambda i: (i, u_col)),
            pl.BlockSpec((tm, kb), lambda i: (i, u_col + 1)),
            pl.BlockSpec((None, 1, kb), lambda i: (layer, 0, 0)),
            pl.BlockSpec((None, groups, SGU_CHUNK, SGU_CHUNK), lambda i: (layer, 0, 0, 0)),
            pl.BlockSpec((None, SGU_CHUNK, groups), lambda i: (layer, 0, 0)),
            pl.BlockSpec((None, ka, n), lambda i: (layer, 0, 0), pipeline_mode=pl.Buffered(1)),
            pl.BlockSpec((None, kb, n), lambda i: (layer, 0, 0), pipeline_mode=pl.Buffered(1)),
            pl.BlockSpec((tm, half), lambda i: (i, g0)),
            pl.BlockSpec((tm, half), lambda i: (i, g0 + 1)),
            pl.BlockSpec((tm, half), lambda i: (i, g0 + 2)),
            pl.BlockSpec((tm, half), lambda i: (i, g0 + 3)),
        ],
        out_specs=pl.BlockSpec((tm, n), lambda i: (i, 0)),
        out_shape=jax.ShapeDtypeStruct((m, n), BF16),
        scratch_shapes=[pltpu.VMEM((ka, n), BF16), pltpu.VMEM((kb, n), BF16)],
        compiler_params=_params("arbitrary"),
        name="merge",
    )(y_a, proj, proj, gain, w_s, b_s_t, w_a, w_b, proj, proj, proj, proj)


def _outproj_kernel(a_ref, w_ref, x_ref, g_ref, xo_ref, h_ref, w_s):
    @pl.when(pl.program_id(0) == 0)
    def _():
        w_s[...] = w_ref[...].astype(BF16)

    x_new = x_ref[...] + _dot(a_ref[...], w_s[...])
    xo_ref[...] = x_new
    h_ref[...] = _rms_scale(x_new, g_ref[...]).astype(h_ref.dtype)


def _out_proj(a, w, x, gain, layer, *, tm=512):
    m, k = a.shape
    n = w.shape[2]
    return pl.pallas_call(
        _outproj_kernel,
        grid=(m // tm,),
        in_specs=[
            pl.BlockSpec((tm, k), lambda i: (i, 0)),
            pl.BlockSpec((None, k, n), lambda i: (layer, 0, 0), pipeline_mode=pl.Buffered(1)),
            pl.BlockSpec((tm, n), lambda i: (i, 0)),
            pl.BlockSpec((None, 1, n), lambda i: (layer, 0, 0)),
        ],
        out_specs=[pl.BlockSpec((tm, n), lambda i: (i, 0)), pl.BlockSpec((tm, n), lambda i: (i, 0))],
        out_shape=[jax.ShapeDtypeStruct((m, n), F32), jax.ShapeDtypeStruct((m, n), BF16)],
        scratch_shapes=[pltpu.VMEM((k, n), BF16)],
        compiler_params=_params("arbitrary"),
        name="out_proj",
    )(a, w, x, gain)


def _ffn_up_kernel(h_ref, wa_ref, wb_ref, cw_ref, cb_ref, o_ref, wa_s, wb_s, a_ref, tail_ref, *, tiles_per_seq):
    tm = h_ref.shape[0]
    i = pl.program_id(1)

    @pl.when(i == 0)
    def _():
        wa_s[...] = wa_ref[...].astype(BF16)
        wb_s[...] = wb_ref[...].astype(BF16)

    seq_start = i % tiles_per_seq == 0

    @pl.when(seq_start)
    def _():
        a_ref[0:HALO_ROWS, :] = jnp.zeros((HALO_ROWS, a_ref.shape[1]), F32)

    @pl.when(jnp.logical_not(seq_start))
    def _():
        a_ref[0:HALO_ROWS, :] = tail_ref[...]

    h = h_ref[...]
    a = _dot(h, wa_s[...])
    a_ref[HALO_ROWS:, :] = a
    conv = (cb_ref[...] + cw_ref[0:1, :] * a_ref[pl.ds(HALO_ROWS - 2, tm), :]
            + cw_ref[1:2, :] * a_ref[pl.ds(HALO_ROWS - 1, tm), :] + cw_ref[2:3, :] * a)
    o_ref[...] = (_gelu(conv) * _dot(h, wb_s[...])).astype(o_ref.dtype)
    tail_ref[...] = a_ref[tm:, :]


def _ffn_up(h, w_up, conv_w, conv_b, layer, *, seq, tm=1024, tn=512):
    m, d = h.shape
    dff = w_up.shape[2] // 2
    nb = dff // tn
    return pl.pallas_call(
        functools.partial(_ffn_up_kernel, tiles_per_seq=seq // tm),
        grid=(nb, m // tm),
        in_specs=[
            pl.BlockSpec((tm, d), lambda j, i: (i, 0)),
            pl.BlockSpec((None, d, tn), lambda j, i: (layer, 0, j)),
            pl.BlockSpec((None, d, tn), lambda j, i: (layer, 0, nb + j)),
            pl.BlockSpec((None, CONV_WIDTH, tn), lambda j, i: (layer, 0, j)),
            pl.BlockSpec((None, 1, tn), lambda j, i: (layer, 0, j)),
        ],
        out_specs=pl.BlockSpec((tm, tn), lambda j, i: (i, j)),
        out_shape=jax.ShapeDtypeStruct((m, dff), BF16),
        scratch_shapes=[
            pltpu.VMEM((d, tn), BF16),
            pltpu.VMEM((d, tn), BF16),
            pltpu.VMEM((HALO_ROWS + tm, tn), F32),
            pltpu.VMEM((HALO_ROWS, tn), F32),
        ],
        compiler_params=_params("parallel", "arbitrary"),
        name="ffn_up",
    )(h, w_up, w_up, conv_w, conv_b)


def _ffn_down_kernel(a_ref, w_ref, x_ref, o_ref):
    o_ref[...] = x_ref[...] + _dot(a_ref[...], w_ref[...])


def _ffn_down(a, w, x, layer, *, tm=1024, tn=512):
    m, k = a.shape
    n = w.shape[2]
    return pl.pallas_call(
        _ffn_down_kernel,
        grid=(m // tm, n // tn),
        in_specs=[
            pl.BlockSpec((tm, k), lambda i, j: (i, 0)),
            pl.BlockSpec((None, k, tn), lambda i, j: (layer, 0, j)),
            pl.BlockSpec((tm, tn), lambda i, j: (i, j)),
        ],
        out_specs=pl.BlockSpec((tm, tn), lambda i, j: (i, j)),
        out_shape=jax.ShapeDtypeStruct((m, n), F32),
        compiler_params=_params("parallel", "parallel"),
        name="ffn_down",
    )(a, w, x)


def _norm_kernel(x_ref, g_ref, o_ref):
    o_ref[...] = _rms_scale(x_ref[...], g_ref[...])


def _final_norm(x, gain, *, tm=1024):
    m, d = x.shape
    return pl.pallas_call(
        _norm_kernel,
        grid=(m // tm,),
        in_specs=[pl.BlockSpec((tm, d), lambda i: (i, 0)), pl.BlockSpec((1, d), lambda i: (0, 0))],
        out_specs=pl.BlockSpec((tm, d), lambda i: (i, 0)),
        out_shape=jax.ShapeDtypeStruct((m, d), F32),
        compiler_params=_params("parallel"),
        name="final_norm",
    )(x, gain)


def kernel(x, g_mix, w_in, b_forget, g_sgu, w_spatial, b_spatial, w_branch_a, w_branch_b, w_out, g_ffn,
           w_up, conv_w, conv_b, w_down, g_final):
    batch, seq, d = x.shape
    depth = w_in.shape[0]
    heads = b_forget.shape[1]
    fox = heads * HEAD_DIM
    sgu_w = g_sgu.shape[1]
    f0 = 3 * fox
    u0 = 3 * fox
    gate0 = u0 + 2 * sgu_w
    assert fox == sgu_w, "column-block indexing of proj assumes equal branch widths"

    w_main, w_f = _stage_w_in(jnp.swapaxes(w_in, 1, 2), f0=f0, gap=heads)
    b_f = jnp.pad(b_forget, ((0, 0), (0, LANES - heads))).reshape(depth, 1, LANES)
    w_down_b = w_down.astype(BF16)
    b_s_t = jnp.swapaxes(b_spatial, 1, 2)
    g_mix3, g_sgu3, g_ffn3 = (g.reshape(depth, 1, -1) for g in (g_mix, g_sgu, g_ffn))
    conv_b3 = conv_b.reshape(depth, 1, -1)

    xs = x.reshape(batch * seq, d)
    for l in range(depth):
        proj, logf = _in_proj(xs, g_mix3, w_main, w_f, b_f, l)
        decay = _fox_decay(logf, batch=batch, heads=heads)
        y_a = _fox_attn(proj, decay, batch=batch, heads=heads)
        merged = _merge(y_a, proj, g_sgu3, w_spatial, b_s_t, w_branch_a, w_branch_b, l, u_col=u0 // sgu_w,
                        gate_col=gate0)
        xs, h_ffn = _out_proj(merged, w_out, xs, g_ffn3, l)
        g = _ffn_up(h_ffn, w_up, conv_w, conv_b3, l, seq=seq)
        xs = _ffn_down(g, w_down_b, xs, l)
    return _final_norm(xs, g_final.reshape(1, d)).reshape(batch, seq, d)
```

```python
import functools
import math

import jax
import jax.numpy as jnp
from jax import lax
from jax.experimental import pallas as pl
from jax.experimental.pallas import tpu as pltpu

F32 = jnp.float32
BF16 = jnp.bfloat16

HEAD_DIM = 128
SGU_GROUP_DIM = 128
SGU_CHUNK = 128
CONV_WIDTH = 3
RMS_EPS = 1e-6
LOG2E = math.log2(math.e)

LANES = 128
AUG_DIM = 2 * HEAD_DIM
HALO_ROWS = 8
VMEM_LIMIT = 56 * 1024 * 1024


def _dot(a, b):
    return jnp.dot(a, b, preferred_element_type=F32)


def _gelu(x):
    a = -2.0 * LOG2E * math.sqrt(2.0 / math.pi)
    return x / (1.0 + jnp.exp2(x * (a + (a * 0.044715) * (x * x))))


def _sigmoid(x):
    return 1.0 / (1.0 + jnp.exp2(-LOG2E * x))


def _rms_scale(x, gain):
    inv = lax.rsqrt(jnp.mean(x * x, axis=-1, keepdims=True) + RMS_EPS)
    return (x * inv) * gain


def _split3(x):
    hi = x.astype(BF16)
    r = x - hi.astype(F32)
    mid = r.astype(BF16)
    lo = (r - mid.astype(F32)).astype(BF16)
    return hi, mid, lo


def _params(*sem):
    return pltpu.CompilerParams(dimension_semantics=sem, vmem_limit_bytes=VMEM_LIMIT)


def _stage_kernel(w_ref, nxt_ref, o_ref, wf_ref, *, first_shifted, gap):
    t = pl.program_id(1)

    @pl.when(t < first_shifted)
    def _():
        o_ref[...] = w_ref[...].T.astype(BF16)

    @pl.when(t >= first_shifted)
    def _():
        o_ref[...] = jnp.concatenate([w_ref[gap:, :], nxt_ref[...]], axis=0).T.astype(BF16)

    @pl.when(t == first_shifted)
    def _():
        row = lax.broadcasted_iota(jnp.int32, (LANES, w_ref.shape[1]), 0)
        w_f = jnp.where(row < gap, w_ref[:LANES, :], 0.0).T
        hi = w_f.astype(BF16)
        wf_ref[:, :LANES] = hi
        wf_ref[:, LANES:] = (w_f - hi.astype(F32)).astype(BF16)


def _stage_w_in(w_in_t, *, f0, gap, tn=1024):
    depth, n_in, d = w_in_t.shape
    n = n_in - gap
    first_shifted = f0 // tn
    assert gap == 8, "the column gap must be one f32 sublane tile"
    return pl.pallas_call(
        functools.partial(_stage_kernel, first_shifted=first_shifted, gap=gap),
        grid=(depth, n // tn),
        in_specs=[
            pl.BlockSpec((None, tn, d), lambda l, t: (l, t, 0)),
            pl.BlockSpec((None, gap, d), lambda l, t: (l, (t + 1) * (tn // gap), 0)),
        ],
        out_specs=[
            pl.BlockSpec((None, d, tn), lambda l, t: (l, 0, t)),
            pl.BlockSpec((None, d, 2 * LANES), lambda l, t: (l, 0, 0)),
        ],
        out_shape=[
            jax.ShapeDtypeStruct((depth, d, n), BF16),
            jax.ShapeDtypeStruct((depth, d, 2 * LANES), BF16),
        ],
        compiler_params=_params("parallel", "arbitrary"),
        name="stage_w_in",
    )(w_in_t, w_in_t)


def _inproj_kernel(x_ref, g_ref, w_ref, wf_ref, bf_ref, o_ref, lf_ref, h_ref, *, rows):
    j = pl.program_id(1)

    @pl.when(j == 0)
    def _():
        for r0 in range(0, x_ref.shape[0], rows):
            sl = slice(r0, r0 + rows)
            hb = _rms_scale(x_ref[sl, :], g_ref[...]).astype(BF16)
            h_ref[sl, :] = hb
            zz = _dot(hb, wf_ref[...])
            z = (zz[:, :LANES] + zz[:, LANES:]) + bf_ref[...]
            lf_ref[sl, :] = jnp.minimum(z, 0.0) - jnp.log(1.0 + jnp.exp(-jnp.abs(z)))
            o_ref[sl, :] = _dot(hb, w_ref[...]).astype(o_ref.dtype)

    @pl.when(j > 0)
    def _():
        o_ref[...] = _dot(h_ref[...], w_ref[...]).astype(o_ref.dtype)


def _in_proj(x, gain, w, w_f, b_f, layer, *, tm=1024, tn=1536, rows=256):
    m, d = x.shape
    n = w.shape[2]
    return pl.pallas_call(
        functools.partial(_inproj_kernel, rows=rows),
        grid=(m // tm, n // tn),
        in_specs=[
            pl.BlockSpec((tm, d), lambda i, j: (i, 0)),
            pl.BlockSpec((None, 1, d), lambda i, j: (layer, 0, 0)),
            pl.BlockSpec((None, d, tn), lambda i, j: (layer, 0, j)),
            pl.BlockSpec((None, d, 2 * LANES), lambda i, j: (layer, 0, 0)),
            pl.BlockSpec((None, 1, LANES), lambda i, j: (layer, 0, 0)),
        ],
        out_specs=[
            pl.BlockSpec((tm, tn), lambda i, j: (i, j)),
            pl.BlockSpec((tm, LANES), lambda i, j: (i, 0)),
        ],
        out_shape=[
            jax.ShapeDtypeStruct((m, n), BF16),
            jax.ShapeDtypeStruct((m, LANES), F32),
        ],
        scratch_shapes=[pltpu.VMEM((tm, d), BF16)],
        compiler_params=_params("parallel", "arbitrary"),
        name="in_proj",
    )(x, gain, w, w_f, b_f)


def _decay_kernel(lf_ref, o_ref, carry_ref, *, heads):
    ts = lf_ref.shape[0]

    @pl.when(pl.program_id(1) == 0)
    def _():
        carry_ref[...] = jnp.zeros_like(carry_ref)

    r = lax.broadcasted_iota(jnp.int32, (ts, ts), 0)
    c = lax.broadcasted_iota(jnp.int32, (ts, ts), 1)
    tri = jnp.where(r >= c, 1.0, 0.0).astype(BF16)
    x1, x2, x3 = _split3(lf_ref[...])
    cum = (_dot(tri, x1) + _dot(tri, x2)) + _dot(tri, x3) + carry_ref[0:1, :]
    carry_ref[0:1, :] = cum[ts - 1:ts, :]

    lane = lax.broadcasted_iota(jnp.int32, cum.shape, 1)
    hi, mid, lo = _split3(jnp.where(lane < heads, cum * LOG2E, 0.0))
    packed = (hi.astype(F32) + pltpu.roll(mid.astype(F32), heads, 1)
              + pltpu.roll(lo.astype(F32), 2 * heads, 1))
    o_ref[...] = packed.astype(BF16)


def _fox_decay(logf, *, batch, heads, ts=512):
    m = logf.shape[0]
    nblk = m // batch // ts
    assert 3 * heads <= LANES
    return pl.pallas_call(
        functools.partial(_decay_kernel, heads=heads),
        grid=(batch, nblk),
        in_specs=[pl.BlockSpec((ts, LANES), lambda b, s: (b * nblk + s, 0))],
        out_specs=pl.BlockSpec((ts, LANES), lambda b, s: (b * nblk + s, 0)),
        out_shape=jax.ShapeDtypeStruct((m, LANES), BF16),
        scratch_shapes=[pltpu.VMEM((8, LANES), F32)],
        compiler_params=_params("parallel", "arbitrary"),
        name="fox_decay",
    )(logf)


def _decay_columns(parts, head0, hg, heads, lane0, sign):
    row = lax.broadcasted_iota(jnp.int32, (LANES, hg * LANES), 0)
    col = lax.broadcasted_iota(jnp.int32, (LANES, hg * LANES), 1)
    term, head = row // heads, row % heads
    hit = (term < 3) & (head == head0 + col // LANES) & (col % LANES == term + lane0)
    return _dot(parts, jnp.where(hit, sign, 0.0).astype(BF16))


def _attn_kernel(q_ref, k_ref, v_ref, dq_ref, dk_ref, o_ref, qa_ref, ka_ref, acc_ref, m_ref, s_ref,
                 *, hg, heads, tq, tk, q_scale, rows):
    i = pl.program_id(2)
    seq = k_ref.shape[0]
    head0 = pl.program_id(1) * hg
    lane_q = lax.broadcasted_iota(jnp.int32, (tq, LANES), 1)
    lane_k = lax.broadcasted_iota(jnp.int32, (rows, LANES), 1)

    @pl.when(i == 0)
    def _():
        def chunk(c, carry):
            sl = pl.ds(pl.multiple_of(c * rows, rows), rows)
            dk = _decay_columns(dk_ref[sl, :], head0, hg, heads, 3, -1.0)
            for h in range(hg):
                ext = jnp.where(lane_k < 3, 1.0, dk[:, h * LANES:(h + 1) * LANES])
                ka_ref[sl, h * AUG_DIM:h * AUG_DIM + HEAD_DIM] = k_ref[sl, h * HEAD_DIM:(h + 1) * HEAD_DIM]
                ka_ref[sl, h * AUG_DIM + HEAD_DIM:(h + 1) * AUG_DIM] = ext.astype(BF16)
            return carry
        lax.fori_loop(0, seq // rows, chunk, 0)

    dq = _decay_columns(dq_ref[...], head0, hg, heads, 0, 1.0)
    for h in range(hg):
        ext = jnp.where((lane_q >= 3) & (lane_q < 6), 1.0, dq[:, h * LANES:(h + 1) * LANES])
        qh = q_ref[:, h * HEAD_DIM:(h + 1) * HEAD_DIM].astype(F32) * q_scale
        qa_ref[:, h * AUG_DIM:h * AUG_DIM + HEAD_DIM] = qh.astype(BF16)
        qa_ref[:, h * AUG_DIM + HEAD_DIM:(h + 1) * AUG_DIM] = ext.astype(BF16)

    lane = lax.broadcasted_iota(jnp.int32, (tk, HEAD_DIM), 1)
    ones_col = jnp.where(lane == 0, 1.0, 0.0).astype(BF16)
    causal_bias = jnp.where(lax.broadcasted_iota(jnp.int32, (tk, tk), 1)
                            <= lax.broadcasted_iota(jnp.int32, (tk, tk), 0), 0.0, -1e30)

    def scores(h, j0, r0):
        q = qa_ref[r0:, h * AUG_DIM:(h + 1) * AUG_DIM]
        k = ka_ref[pl.ds(j0, tk), h * AUG_DIM:(h + 1) * AUG_DIM]
        return lax.dot_general(q, k, (((1,), (1,)), ((), ())), preferred_element_type=F32)

    def fold(h, s, j0, r0, masked):
        v = v_ref[pl.ds(j0, tk), h * HEAD_DIM:(h + 1) * HEAD_DIM]
        if masked:
            top = s[:tk] + causal_bias
            s = top if s.shape[0] == tk else jnp.concatenate([top, s[tk:]], axis=0)
        m_prev = m_ref[h, r0:, :]
        m_new = jnp.maximum(m_prev, jnp.max(s, axis=-1, keepdims=True))
        m_ref[h, r0:, :] = m_new
        p = jnp.exp2(s - jnp.tile(m_new, (1, tk // LANES))).astype(BF16)
        pv = _dot(p, jnp.concatenate([v, ones_col], axis=-1))
        alpha = jnp.tile(jnp.exp2(m_prev - m_new), (1, AUG_DIM // LANES))
        acc_ref[h, r0:, :] = acc_ref[h, r0:, :] * alpha + pv

    for h in range(hg):
        s_ref[h] = scores(h, 0, 0)
    m_ref[...] = jnp.full(m_ref.shape, -1e30, F32)
    acc_ref[...] = jnp.zeros(acc_ref.shape, F32)

    def body(j, carry):
        j0 = pl.multiple_of(j * tk, tk)
        for h in range(hg):
            s = s_ref[h]
            s_ref[h] = scores(h, j0 + tk, 0)
            fold(h, s, j0, 0, False)
        return carry

    lax.fori_loop(0, i * (tq // tk), body, 0)
    nd = tq // tk
    for d in range(nd):
        j0 = pl.multiple_of(i * tq + d * tk, tk)
        r0 = d * tk
        for h in range(hg):
            s = s_ref[h, r0:, :]
            if d + 1 < nd:
                s_ref[h, r0 + tk:, :] = scores(h, j0 + tk, r0 + tk)
            fold(h, s, j0, r0, True)
    for h in range(hg):
        acc = acc_ref[h]
        out = acc[:, :HEAD_DIM] / acc[:, HEAD_DIM:HEAD_DIM + 1]
        o_ref[:, h * HEAD_DIM:(h + 1) * HEAD_DIM] = out.astype(o_ref.dtype)


def _fox_attn(proj, decay, *, batch, heads, hg=4, tq=1024, tk=512, rows=512):
    m = proj.shape[0]
    seq = m // batch
    nq = seq // tq
    ng = heads // hg
    return pl.pallas_call(
        functools.partial(_attn_kernel, hg=hg, heads=heads, tq=tq, tk=tk,
                          q_scale=HEAD_DIM ** -0.5 * LOG2E, rows=rows),
        grid=(batch, ng, nq),
        in_specs=[
            pl.BlockSpec((tq, hg * HEAD_DIM), lambda b, g, i: (b * nq + i, g)),
            pl.BlockSpec((seq, hg * HEAD_DIM), lambda b, g, i: (b, ng + g), pipeline_mode=pl.Buffered(1)),
            pl.BlockSpec((seq, hg * HEAD_DIM), lambda b, g, i: (b, 2 * ng + g)),
            pl.BlockSpec((tq, LANES), lambda b, g, i: (b * nq + i, 0)),
            pl.BlockSpec((seq, LANES), lambda b, g, i: (b, 0), pipeline_mode=pl.Buffered(1)),
        ],
        out_specs=pl.BlockSpec((tq, hg * HEAD_DIM), lambda b, g, i: (b * nq + i, g)),
        out_shape=jax.ShapeDtypeStruct((m, heads * HEAD_DIM), BF16),
        scratch_shapes=[
            pltpu.VMEM((tq, hg * AUG_DIM), BF16),
            pltpu.VMEM((seq, hg * AUG_DIM), BF16),
            pltpu.VMEM((hg, tq, AUG_DIM), F32),
            pltpu.VMEM((hg, tq, LANES), F32),
            pltpu.VMEM((hg, tq, tk), F32),
        ],
        compiler_params=_params("parallel", "parallel", "arbitrary"),
        name="fox_attn",
    )(proj, proj, proj, decay, decay)


def _spatial_gating(u_ref, v_ref, g_ref, w_ref, bt_ref, *, groups):
    ts = u_ref.shape[0]
    vn = _rms_scale(_gelu(v_ref[...].astype(F32)), g_ref[...]).astype(BF16)
    r = lax.broadcasted_iota(jnp.int32, (SGU_CHUNK, SGU_CHUNK), 0)
    c = lax.broadcasted_iota(jnp.int32, (SGU_CHUNK, SGU_CHUNK), 1)
    out = []
    for g in range(groups):
        w = jnp.where(r >= c, w_ref[g], 0.0).astype(BF16)
        bias = bt_ref[:, g:g + 1]
        cols = slice(g * SGU_GROUP_DIM, (g + 1) * SGU_GROUP_DIM)
        chunks = []
        for ch in range(ts // SGU_CHUNK):
            rows = slice(ch * SGU_CHUNK, (ch + 1) * SGU_CHUNK)
            mixed = _dot(w, vn[rows, cols]) + bias
            chunks.append((_gelu(u_ref[rows, cols].astype(F32)) * mixed).astype(BF16))
        out.append(jnp.concatenate(chunks, axis=0))
    return jnp.concatenate(out, axis=1)


def _merge_kernel(ya_ref, u_ref, v_ref, gs_ref, ws_ref, bt_ref, wa_ref, wb_ref, ga0_ref, ga1_ref,
                  gb0_ref, gb1_ref, o_ref, wa_s, wb_s, *, groups):
    @pl.when(pl.program_id(0) == 0)
    def _():
        wa_s[...] = wa_ref[...].astype(BF16)
        wb_s[...] = wb_ref[...].astype(BF16)

    a = _dot(ya_ref[...], wa_s[...])
    y_b = _spatial_gating(u_ref, v_ref, gs_ref, ws_ref, bt_ref, groups=groups)
    gate_a = jnp.concatenate([ga0_ref[...], ga1_ref[...]], axis=1).astype(F32)
    gate_b = jnp.concatenate([gb0_ref[...], gb1_ref[...]], axis=1).astype(F32)
    b = _sigmoid(gate_b) * _dot(y_b, wb_s[...])
    o_ref[...] = (_sigmoid(gate_a) * a + b).astype(o_ref.dtype)


def _merge(y_a, proj, gain, w_s, b_s_t, w_a, w_b, layer, *, u_col, gate_col, tm=512):
    m, ka = y_a.shape
    groups = w_s.shape[1]
    kb = groups * SGU_GROUP_DIM
    n = w_a.shape[2]
    half = n // 2
    g0 = gate_col // half
    return pl.pallas_call(
        functools.partial(_merge_kernel, groups=groups),
        grid=(m // tm,),
        in_specs=[
            pl.BlockSpec((tm, ka), lambda i: (i, 0)),
            pl.BlockSpec((tm, kb), lambda i: (i, u_col)),
            pl.BlockSpec((tm, kb), lambda i: (i, u_col + 1)),
            pl.BlockSpec((None, 1, kb), lambda i: (layer, 0, 0)),
            pl.BlockSpec((None, groups, SGU_CHUNK, SGU_CHUNK), lambda i: (layer, 0, 0, 0)),
            pl.BlockSpec((None, SGU_CHUNK, groups), lambda i: (layer, 0, 0)),
            pl.BlockSpec((None, ka, n), lambda i: (layer, 0, 0), pipeline_mode=pl.Buffered(1)),
            pl.BlockSpec((None, kb, n), lambda i: (layer, 0, 0), pipeline_mode=pl.Buffered(1)),
            pl.BlockSpec((tm, half), lambda i: (i, g0)),
            pl.BlockSpec((tm, half), lambda i: (i, g0 + 1)),
            pl.BlockSpec((tm, half), lambda i: (i, g0 + 2)),
            pl.BlockSpec((tm, half), lambda i: (i, g0 + 3)),
        ],
        out_specs=pl.BlockSpec((tm, n), lambda i: (i, 0)),
        out_shape=jax.ShapeDtypeStruct((m, n), BF16),
        scratch_shapes=[pltpu.VMEM((ka, n), BF16), pltpu.VMEM((kb, n), BF16)],
        compiler_params=_params("arbitrary"),
        name="merge",
    )(y_a, proj, proj, gain, w_s, b_s_t, w_a, w_b, proj, proj, proj, proj)


def _outproj_kernel(a_ref, w_ref, x_ref, g_ref, xo_ref, h_ref, w_s):
    @pl.when(pl.program_id(0) == 0)
    def _():
        w_s[...] = w_ref[...].astype(BF16)

    x_new = x_ref[...] + _dot(a_ref[...], w_s[...])
    xo_ref[...] = x_new
    h_ref[...] = _rms_scale(x_new, g_ref[...]).astype(h_ref.dtype)


def _out_proj(a, w, x, gain, layer, *, tm=512):
    m, k = a.shape
    n = w.shape[2]
    return pl.pallas_call(
        _outproj_kernel,
        grid=(m // tm,),
        in_specs=[
            pl.BlockSpec((tm, k), lambda i: (i, 0)),
            pl.BlockSpec((None, k, n), lambda i: (layer, 0, 0), pipeline_mode=pl.Buffered(1)),
            pl.BlockSpec((tm, n), lambda i: (i, 0)),
            pl.BlockSpec((None, 1, n), lambda i: (layer, 0, 0)),
        ],
        out_specs=[pl.BlockSpec((tm, n), lambda i: (i, 0)), pl.BlockSpec((tm, n), lambda i: (i, 0))],
        out_shape=[jax.ShapeDtypeStruct((m, n), F32), jax.ShapeDtypeStruct((m, n), BF16)],
        scratch_shapes=[pltpu.VMEM((k, n), BF16)],
        compiler_params=_params("arbitrary"),
        name="out_proj",
    )(a, w, x, gain)


def _ffn_up_kernel(h_ref, wa_ref, wb_ref, cw_ref, cb_ref, o_ref, wa_s, wb_s, a_ref, tail_ref, *, tiles_per_seq):
    tm = h_ref.shape[0]
    i = pl.program_id(1)

    @pl.when(i == 0)
    def _():
        wa_s[...] = wa_ref[...].astype(BF16)
        wb_s[...] = wb_ref[...].astype(BF16)

    seq_start = i % tiles_per_seq == 0

    @pl.when(seq_start)
    def _():
        a_ref[0:HALO_ROWS, :] = jnp.zeros((HALO_ROWS, a_ref.shape[1]), F32)

    @pl.when(jnp.logical_not(seq_start))
    def _():
        a_ref[0:HALO_ROWS, :] = tail_ref[...]

    h = h_ref[...]
    a = _dot(h, wa_s[...])
    a_ref[HALO_ROWS:, :] = a
    conv = (cb_ref[...] + cw_ref[0:1, :] * a_ref[pl.ds(HALO_ROWS - 2, tm), :]
            + cw_ref[1:2, :] * a_ref[pl.ds(HALO_ROWS - 1, tm), :] + cw_ref[2:3, :] * a)
    o_ref[...] = (_gelu(conv) * _dot(h, wb_s[...])).astype(o_ref.dtype)
    tail_ref[...] = a_ref[tm:, :]


def _ffn_up(h, w_up, conv_w, conv_b, layer, *, seq, tm=1024, tn=512):
    m, d = h.shape
    dff = w_up.shape[2] // 2
    nb = dff // tn
    return pl.pallas_call(
        functools.partial(_ffn_up_kernel, tiles_per_seq=seq // tm),
        grid=(nb, m // tm),
        in_specs=[
            pl.BlockSpec((tm, d), lambda j, i: (i, 0)),
            pl.BlockSpec((None, d, tn), lambda j, i: (layer, 0, j)),
            pl.BlockSpec((None, d, tn), lambda j, i: (layer, 0, nb + j)),
            pl.BlockSpec((None, CONV_WIDTH, tn), lambda j, i: (layer, 0, j)),
            pl.BlockSpec((None, 1, tn), lambda j, i: (layer, 0, j)),
        ],
        out_specs=pl.BlockSpec((tm, tn), lambda j, i: (i, j)),
        out_shape=jax.ShapeDtypeStruct((m, dff), BF16),
        scratch_shapes=[
            pltpu.VMEM((d, tn), BF16),
            pltpu.VMEM((d, tn), BF16),
            pltpu.VMEM((HALO_ROWS + tm, tn), F32),
            pltpu.VMEM((HALO_ROWS, tn), F32),
        ],
        compiler_params=_params("parallel", "arbitrary"),
        name="ffn_up",
    )(h, w_up, w_up, conv_w, conv_b)


def _ffn_down_kernel(a_ref, w_ref, x_ref, o_ref):
    o_ref[...] = x_ref[...] + _dot(a_ref[...], w_ref[...])


def _ffn_down(a, w, x, layer, *, tm=1024, tn=512):
    m, k = a.shape
    n = w.shape[2]
    return pl.pallas_call(
        _ffn_down_kernel,
        grid=(m // tm, n // tn),
        in_specs=[
            pl.BlockSpec((tm, k), lambda i, j: (i, 0)),
            pl.BlockSpec((None, k, tn), lambda i, j: (layer, 0, j)),
            pl.BlockSpec((tm, tn), lambda i, j: (i, j)),
        ],
        out_specs=pl.BlockSpec((tm, tn), lambda i, j: (i, j)),
        out_shape=jax.ShapeDtypeStruct((m, n), F32),
        compiler_params=_params("parallel", "parallel"),
        name="ffn_down",
    )(a, w, x)


def _ffn_down_norm_kernel(a_ref, w_ref, x_ref, g_ref, o_ref, *, rows):
    k = pl.program_id(1)

    @pl.when(k == 0)
    def _():
        o_ref[...] = x_ref[...]

    o_ref[...] += _dot(a_ref[...], w_ref[...])

    @pl.when(k == pl.num_programs(1) - 1)
    def _():
        def chunk(c, carry):
            sl = pl.ds(pl.multiple_of(c * rows, rows), rows)
            o_ref[sl, :] = _rms_scale(o_ref[sl, :], g_ref[...])
            return carry
        lax.fori_loop(0, o_ref.shape[0] // rows, chunk, 0)


def _ffn_down_norm(a, w, x, gain, layer, *, tm=1024, tk=512, rows=256):
    m, k = a.shape
    n = w.shape[2]
    return pl.pallas_call(
        functools.partial(_ffn_down_norm_kernel, rows=rows),
        grid=(m // tm, k // tk),
        in_specs=[
            pl.BlockSpec((tm, tk), lambda i, j: (i, j)),
            pl.BlockSpec((None, tk, n), lambda i, j: (layer, j, 0)),
            pl.BlockSpec((tm, n), lambda i, j: (i, 0)),
            pl.BlockSpec((1, n), lambda i, j: (0, 0)),
        ],
        out_specs=pl.BlockSpec((tm, n), lambda i, j: (i, 0)),
        out_shape=jax.ShapeDtypeStruct((m, n), F32),
        compiler_params=_params("parallel", "arbitrary"),
        name="ffn_down_norm",
    )(a, w, x, gain)


def kernel(x, g_mix, w_in, b_forget, g_sgu, w_spatial, b_spatial, w_branch_a, w_branch_b, w_out, g_ffn,
           w_up, conv_w, conv_b, w_down, g_final):
    batch, seq, d = x.shape
    depth = w_in.shape[0]
    heads = b_forget.shape[1]
    fox = heads * HEAD_DIM
    sgu_w = g_sgu.shape[1]
    f0 = 3 * fox
    u0 = 3 * fox
    gate0 = u0 + 2 * sgu_w
    assert fox == sgu_w, "column-block indexing of proj assumes equal branch widths"

    w_main, w_f = _stage_w_in(jnp.swapaxes(w_in, 1, 2), f0=f0, gap=heads)
    b_f = jnp.pad(b_forget, ((0, 0), (0, LANES - heads))).reshape(depth, 1, LANES)
    w_down_b = w_down.astype(BF16)
    b_s_t = jnp.swapaxes(b_spatial, 1, 2)
    g_mix3, g_sgu3, g_ffn3 = (g.reshape(depth, 1, -1) for g in (g_mix, g_sgu, g_ffn))
    conv_b3 = conv_b.reshape(depth, 1, -1)

    xs = x.reshape(batch * seq, d)
    for l in range(depth):
        proj, logf = _in_proj(xs, g_mix3, w_main, w_f, b_f, l)
        decay = _fox_decay(logf, batch=batch, heads=heads)
        y_a = _fox_attn(proj, decay, batch=batch, heads=heads)
        merged = _merge(y_a, proj, g_sgu3, w_spatial, b_s_t, w_branch_a, w_branch_b, l, u_col=u0 // sgu_w,
                        gate_col=gate0)
        xs, h_ffn = _out_proj(merged, w_out, xs, g_ffn3, l)
        g = _ffn_up(h_ffn, w_up, conv_w, conv_b3, l, seq=seq)
        if l + 1 < depth:
            xs = _ffn_down(g, w_down_b, xs, l)
        else:
            xs = _ffn_down_norm(g, w_down_b, xs, g_final.reshape(1, d), l)
    return xs.reshape(batch, seq, d)
```

```python
import functools
import math

import jax
import jax.numpy as jnp
from jax import lax
from jax.experimental import pallas as pl
from jax.experimental.pallas import tpu as pltpu

F32 = jnp.float32
BF16 = jnp.bfloat16

HEAD_DIM = 128
SGU_GROUP_DIM = 128
SGU_CHUNK = 128
CONV_WIDTH = 3
RMS_EPS = 1e-6
LOG2E = math.log2(math.e)

LANES = 128
AUG_DIM = 2 * HEAD_DIM
HALO_ROWS = 8
VMEM_LIMIT = 56 * 1024 * 1024


def _dot(a, b):
    return jnp.dot(a, b, preferred_element_type=F32)


def _gelu(x):
    a = -2.0 * LOG2E * math.sqrt(2.0 / math.pi)
    return x / (1.0 + jnp.exp2(x * (a + (a * 0.044715) * (x * x))))


def _sigmoid(x):
    return 1.0 / (1.0 + jnp.exp2(-LOG2E * x))


def _rms_scale(x, gain):
    inv = lax.rsqrt(jnp.mean(x * x, axis=-1, keepdims=True) + RMS_EPS)
    return (x * inv) * gain


def _split3(x):
    hi = x.astype(BF16)
    r = x - hi.astype(F32)
    mid = r.astype(BF16)
    lo = (r - mid.astype(F32)).astype(BF16)
    return hi, mid, lo


def _params(*sem):
    return pltpu.CompilerParams(dimension_semantics=sem, vmem_limit_bytes=VMEM_LIMIT)


def _stage_kernel(w_ref, nxt_ref, o_ref, wf_ref, *, first_shifted, gap):
    t = pl.program_id(1)

    @pl.when(t < first_shifted)
    def _():
        o_ref[...] = w_ref[...].T.astype(BF16)

    @pl.when(t >= first_shifted)
    def _():
        o_ref[...] = jnp.concatenate([w_ref[gap:, :], nxt_ref[...]], axis=0).T.astype(BF16)

    @pl.when(t == first_shifted)
    def _():
        row = lax.broadcasted_iota(jnp.int32, (LANES, w_ref.shape[1]), 0)
        w_f = jnp.where(row < gap, w_ref[:LANES, :], 0.0).T
        hi = w_f.astype(BF16)
        wf_ref[:, :LANES] = hi
        wf_ref[:, LANES:] = (w_f - hi.astype(F32)).astype(BF16)


def _stage_w_in(w_in_t, *, f0, gap, tn=1024):
    depth, n_in, d = w_in_t.shape
    n = n_in - gap
    first_shifted = f0 // tn
    assert gap == 8, "the column gap must be one f32 sublane tile"
    return pl.pallas_call(
        functools.partial(_stage_kernel, first_shifted=first_shifted, gap=gap),
        grid=(depth, n // tn),
        in_specs=[
            pl.BlockSpec((None, tn, d), lambda l, t: (l, t, 0)),
            pl.BlockSpec((None, gap, d), lambda l, t: (l, (t + 1) * (tn // gap), 0)),
        ],
        out_specs=[
            pl.BlockSpec((None, d, tn), lambda l, t: (l, 0, t)),
            pl.BlockSpec((None, d, 2 * LANES), lambda l, t: (l, 0, 0)),
        ],
        out_shape=[
            jax.ShapeDtypeStruct((depth, d, n), BF16),
            jax.ShapeDtypeStruct((depth, d, 2 * LANES), BF16),
        ],
        compiler_params=_params("parallel", "arbitrary"),
        name="stage_w_in",
    )(w_in_t, w_in_t)


def _inproj_kernel(x_ref, g_ref, w_ref, wf_ref, bf_ref, o_ref, lf_ref, h_ref, *, rows):
    j = pl.program_id(1)

    @pl.when(j == 0)
    def _():
        for r0 in range(0, x_ref.shape[0], rows):
            sl = slice(r0, r0 + rows)
            hb = _rms_scale(x_ref[sl, :], g_ref[...]).astype(BF16)
            h_ref[sl, :] = hb
            zz = _dot(hb, wf_ref[...])
            z = (zz[:, :LANES] + zz[:, LANES:]) + bf_ref[...]
            lf_ref[sl, :] = jnp.minimum(z, 0.0) - jnp.log(1.0 + jnp.exp(-jnp.abs(z)))
            o_ref[sl, :] = _dot(hb, w_ref[...]).astype(o_ref.dtype)

    @pl.when(j > 0)
    def _():
        o_ref[...] = _dot(h_ref[...], w_ref[...]).astype(o_ref.dtype)


def _in_proj(x, gain, w, w_f, b_f, layer, *, tm=1024, tn=1536, rows=256):
    m, d = x.shape
    n = w.shape[2]
    return pl.pallas_call(
        functools.partial(_inproj_kernel, rows=rows),
        grid=(m // tm, n // tn),
        in_specs=[
            pl.BlockSpec((tm, d), lambda i, j: (i, 0)),
            pl.BlockSpec((None, 1, d), lambda i, j: (layer, 0, 0)),
            pl.BlockSpec((None, d, tn), lambda i, j: (layer, 0, j)),
            pl.BlockSpec((None, d, 2 * LANES), lambda i, j: (layer, 0, 0)),
            pl.BlockSpec((None, 1, LANES), lambda i, j: (layer, 0, 0)),
        ],
        out_specs=[
            pl.BlockSpec((tm, tn), lambda i, j: (i, j)),
            pl.BlockSpec((tm, LANES), lambda i, j: (i, 0)),
        ],
        out_shape=[
            jax.ShapeDtypeStruct((m, n), BF16),
            jax.ShapeDtypeStruct((m, LANES), F32),
        ],
        scratch_shapes=[pltpu.VMEM((tm, d), BF16)],
        compiler_params=_params("parallel", "arbitrary"),
        name="in_proj",
    )(x, gain, w, w_f, b_f)


def _decay_kernel(lf_ref, o_ref, carry_ref, *, heads):
    ts = lf_ref.shape[0]

    @pl.when(pl.program_id(1) == 0)
    def _():
        carry_ref[...] = jnp.zeros_like(carry_ref)

    r = lax.broadcasted_iota(jnp.int32, (ts, ts), 0)
    c = lax.broadcasted_iota(jnp.int32, (ts, ts), 1)
    tri = jnp.where(r >= c, 1.0, 0.0).astype(BF16)
    x1, x2, x3 = _split3(lf_ref[...])
    cum = (_dot(tri, x1) + _dot(tri, x2)) + _dot(tri, x3) + carry_ref[0:1, :]
    carry_ref[0:1, :] = cum[ts - 1:ts, :]

    lane = lax.broadcasted_iota(jnp.int32, cum.shape, 1)
    hi, mid, lo = _split3(jnp.where(lane < heads, cum * LOG2E, 0.0))
    packed = (hi.astype(F32) + pltpu.roll(mid.astype(F32), heads, 1)
              + pltpu.roll(lo.astype(F32), 2 * heads, 1))
    o_ref[...] = packed.astype(BF16)


def _fox_decay(logf, *, batch, heads, ts=512):
    m = logf.shape[0]
    nblk = m // batch // ts
    assert 3 * heads <= LANES
    return pl.pallas_call(
        functools.partial(_decay_kernel, heads=heads),
        grid=(batch, nblk),
        in_specs=[pl.BlockSpec((ts, LANES), lambda b, s: (b * nblk + s, 0))],
        out_specs=pl.BlockSpec((ts, LANES), lambda b, s: (b * nblk + s, 0)),
        out_shape=jax.ShapeDtypeStruct((m, LANES), BF16),
        scratch_shapes=[pltpu.VMEM((8, LANES), F32)],
        compiler_params=_params("parallel", "arbitrary"),
        name="fox_decay",
    )(logf)


def _decay_columns(parts, head0, hg, heads, lane0, sign):
    row = lax.broadcasted_iota(jnp.int32, (LANES, hg * LANES), 0)
    col = lax.broadcasted_iota(jnp.int32, (LANES, hg * LANES), 1)
    term, head = row // heads, row % heads
    hit = (term < 3) & (head == head0 + col // LANES) & (col % LANES == term + lane0)
    return _dot(parts, jnp.where(hit, sign, 0.0).astype(BF16))


def _attn_kernel(q_ref, k_ref, v_ref, dq_ref, dk_ref, o_ref, qa_ref, ka_ref, acc_ref, m_ref, s_ref,
                 *, hg, heads, tq, tk, q_scale, rows):
    i = pl.program_id(2)
    seq = k_ref.shape[0]
    head0 = pl.program_id(1) * hg
    lane_q = lax.broadcasted_iota(jnp.int32, (tq, LANES), 1)
    lane_k = lax.broadcasted_iota(jnp.int32, (rows, LANES), 1)

    @pl.when(i == 0)
    def _():
        def chunk(c, carry):
            sl = pl.ds(pl.multiple_of(c * rows, rows), rows)
            dk = _decay_columns(dk_ref[sl, :], head0, hg, heads, 3, -1.0)
            for h in range(hg):
                ext = jnp.where(lane_k < 3, 1.0, dk[:, h * LANES:(h + 1) * LANES])
                ka_ref[sl, h * AUG_DIM:h * AUG_DIM + HEAD_DIM] = k_ref[sl, h * HEAD_DIM:(h + 1) * HEAD_DIM]
                ka_ref[sl, h * AUG_DIM + HEAD_DIM:(h + 1) * AUG_DIM] = ext.astype(BF16)
            return carry
        lax.fori_loop(0, seq // rows, chunk, 0)

    dq = _decay_columns(dq_ref[...], head0, hg, heads, 0, 1.0)
    for h in range(hg):
        ext = jnp.where((lane_q >= 3) & (lane_q < 6), 1.0, dq[:, h * LANES:(h + 1) * LANES])
        qh = q_ref[:, h * HEAD_DIM:(h + 1) * HEAD_DIM].astype(F32) * q_scale
        qa_ref[:, h * AUG_DIM:h * AUG_DIM + HEAD_DIM] = qh.astype(BF16)
        qa_ref[:, h * AUG_DIM + HEAD_DIM:(h + 1) * AUG_DIM] = ext.astype(BF16)

    lane = lax.broadcasted_iota(jnp.int32, (tk, HEAD_DIM), 1)
    ones_col = jnp.where(lane == 0, 1.0, 0.0).astype(BF16)
    causal_bias = jnp.where(lax.broadcasted_iota(jnp.int32, (tk, tk), 1)
                            <= lax.broadcasted_iota(jnp.int32, (tk, tk), 0), 0.0, -1e30)

    def scores(h, j0, r0):
        q = qa_ref[r0:, h * AUG_DIM:(h + 1) * AUG_DIM]
        k = ka_ref[pl.ds(j0, tk), h * AUG_DIM:(h + 1) * AUG_DIM]
        return lax.dot_general(q, k, (((1,), (1,)), ((), ())), preferred_element_type=F32)

    def fold(h, s, j0, r0, masked):
        v = v_ref[pl.ds(j0, tk), h * HEAD_DIM:(h + 1) * HEAD_DIM]
        if masked:
            top = s[:tk] + causal_bias
            s = top if s.shape[0] == tk else jnp.concatenate([top, s[tk:]], axis=0)
        m_prev = m_ref[h, r0:, :]
        m_new = jnp.maximum(m_prev, jnp.max(s, axis=-1, keepdims=True))
        m_ref[h, r0:, :] = m_new
        p = jnp.exp2(s - jnp.tile(m_new, (1, tk // LANES))).astype(BF16)
        pv = _dot(p, jnp.concatenate([v, ones_col], axis=-1))
        alpha = jnp.tile(jnp.exp2(m_prev - m_new), (1, AUG_DIM // LANES))
        acc_ref[h, r0:, :] = acc_ref[h, r0:, :] * alpha + pv

    for h in range(hg):
        s_ref[h] = scores(h, 0, 0)
    m_ref[...] = jnp.full(m_ref.shape, -1e30, F32)
    acc_ref[...] = jnp.zeros(acc_ref.shape, F32)

    def body(j, carry):
        j0 = pl.multiple_of(j * tk, tk)
        for h in range(hg):
            s = s_ref[h]
            s_ref[h] = scores(h, j0 + tk, 0)
            fold(h, s, j0, 0, False)
        return carry

    lax.fori_loop(0, i * (tq // tk), body, 0)
    nd = tq // tk
    for d in range(nd):
        j0 = pl.multiple_of(i * tq + d * tk, tk)
        r0 = d * tk
        for h in range(hg):
            s = s_ref[h, r0:, :]
            if d + 1 < nd:
                s_ref[h, r0 + tk:, :] = scores(h, j0 + tk, r0 + tk)
            fold(h, s, j0, r0, True)
    for h in range(hg):
        acc = acc_ref[h]
        out = acc[:, :HEAD_DIM] / acc[:, HEAD_DIM:HEAD_DIM + 1]
        o_ref[:, h * HEAD_DIM:(h + 1) * HEAD_DIM] = out.astype(o_ref.dtype)


def _fox_attn(proj, decay, *, batch, heads, hg=4, tq=1024, tk=512, rows=512):
    m = proj.shape[0]
    seq = m // batch
    nq = seq // tq
    ng = heads // hg
    return pl.pallas_call(
        functools.partial(_attn_kernel, hg=hg, heads=heads, tq=tq, tk=tk,
                          q_scale=HEAD_DIM ** -0.5 * LOG2E, rows=rows),
        grid=(batch, ng, nq),
        in_specs=[
            pl.BlockSpec((tq, hg * HEAD_DIM), lambda b, g, i: (b * nq + i, g)),
            pl.BlockSpec((seq, hg * HEAD_DIM), lambda b, g, i: (b, ng + g), pipeline_mode=pl.Buffered(1)),
            pl.BlockSpec((seq, hg * HEAD_DIM), lambda b, g, i: (b, 2 * ng + g)),
            pl.BlockSpec((tq, LANES), lambda b, g, i: (b * nq + i, 0)),
            pl.BlockSpec((seq, LANES), lambda b, g, i: (b, 0), pipeline_mode=pl.Buffered(1)),
        ],
        out_specs=pl.BlockSpec((tq, hg * HEAD_DIM), lambda b, g, i: (b * nq + i, g)),
        out_shape=jax.ShapeDtypeStruct((m, heads * HEAD_DIM), BF16),
        scratch_shapes=[
            pltpu.VMEM((tq, hg * AUG_DIM), BF16),
            pltpu.VMEM((seq, hg * AUG_DIM), BF16),
            pltpu.VMEM((hg, tq, AUG_DIM), F32),
            pltpu.VMEM((hg, tq, LANES), F32),
            pltpu.VMEM((hg, tq, tk), F32),
        ],
        compiler_params=_params("parallel", "parallel", "arbitrary"),
        name="fox_attn",
    )(proj, proj, proj, decay, decay)


def _spatial_gating(u_ref, v_ref, g_ref, w_ref, bt_ref, *, groups):
    ts = u_ref.shape[0]
    vn = _rms_scale(_gelu(v_ref[...].astype(F32)), g_ref[...]).astype(BF16)
    r = lax.broadcasted_iota(jnp.int32, (SGU_CHUNK, SGU_CHUNK), 0)
    c = lax.broadcasted_iota(jnp.int32, (SGU_CHUNK, SGU_CHUNK), 1)
    out = []
    for g in range(groups):
        w = jnp.where(r >= c, w_ref[g], 0.0).astype(BF16)
        bias = bt_ref[:, g:g + 1]
        cols = slice(g * SGU_GROUP_DIM, (g + 1) * SGU_GROUP_DIM)
        chunks = []
        for ch in range(ts // SGU_CHUNK):
            rows = slice(ch * SGU_CHUNK, (ch + 1) * SGU_CHUNK)
            mixed = _dot(w, vn[rows, cols]) + bias
            chunks.append((_gelu(u_ref[rows, cols].astype(F32)) * mixed).astype(BF16))
        out.append(jnp.concatenate(chunks, axis=0))
    return jnp.concatenate(out, axis=1)


def _merge_kernel(ya_ref, u_ref, v_ref, gs_ref, ws_ref, bt_ref, wa_ref, wb_ref, ga0_ref, ga1_ref,
                  gb0_ref, gb1_ref, o_ref, wa_s, wb_s, *, groups):
    @pl.when(pl.program_id(0) == 0)
    def _():
        wa_s[...] = wa_ref[...].astype(BF16)
        wb_s[...] = wb_ref[...].astype(BF16)

    a = _dot(ya_ref[...], wa_s[...])
    y_b = _spatial_gating(u_ref, v_ref, gs_ref, ws_ref, bt_ref, groups=groups)
    gate_a = jnp.concatenate([ga0_ref[...], ga1_ref[...]], axis=1).astype(F32)
    gate_b = jnp.concatenate([gb0_ref[...], gb1_ref[...]], axis=1).astype(F32)
    b = _sigmoid(gate_b) * _dot(y_b, wb_s[...])
    o_ref[...] = (_sigmoid(gate_a) * a + b).astype(o_ref.dtype)


def _merge(y_a, proj, gain, w_s, b_s_t, w_a, w_b, layer, *, u_col, gate_col, tm=512):
    m, ka = y_a.shape
    groups = w_s.shape[1]
    kb = groups * SGU_GROUP_DIM
    n = w_a.shape[2]
    half = n // 2
    g0 = gate_col // half
    return pl.pallas_call(
        functools.partial(_merge_kernel, groups=groups),
        grid=(m // tm,),
        in_specs=[
            pl.BlockSpec((tm, ka), lambda i: (i, 0)),
            pl.BlockSpec((tm, kb), lambda i: (i, u_col)),
            pl.BlockSpec((tm, kb), lambda i: (i, u_col + 1)),
            pl.BlockSpec((None, 1, kb), lambda i: (layer, 0, 0)),
            pl.BlockSpec((None, groups, SGU_CHUNK, SGU_CHUNK), lambda i: (layer, 0, 0, 0)),
            pl.BlockSpec((None, SGU_CHUNK, groups), lambda i: (layer, 0, 0)),
            pl.BlockSpec((None, ka, n), lambda i: (layer, 0, 0), pipeline_mode=pl.Buffered(1)),
            pl.BlockSpec((None, kb, n), lambda i: (layer, 0, 0), pipeline_mode=pl.Buffered(1)),
            pl.BlockSpec((tm, half), lambda i: (i, g0)),
            pl.BlockSpec((tm, half), lambda i: (i, g0 + 1)),
            pl.BlockSpec((tm, half), lambda i: (i, g0 + 2)),
            pl.BlockSpec((tm, half), lambda i: (i, g0 + 3)),
        ],
        out_specs=pl.BlockSpec((tm, n), lambda i: (i, 0)),
        out_shape=jax.ShapeDtypeStruct((m, n), BF16),
        scratch_shapes=[pltpu.VMEM((ka, n), BF16), pltpu.VMEM((kb, n), BF16)],
        compiler_params=_params("arbitrary"),
        name="merge",
    )(y_a, proj, proj, gain, w_s, b_s_t, w_a, w_b, proj, proj, proj, proj)


def _outproj_kernel(a_ref, w_ref, x_ref, g_ref, xo_ref, h_ref, w_s):
    @pl.when(pl.program_id(0) == 0)
    def _():
        w_s[...] = w_ref[...].astype(BF16)

    x_new = x_ref[...] + _dot(a_ref[...], w_s[...])
    xo_ref[...] = x_new
    h_ref[...] = _rms_scale(x_new, g_ref[...]).astype(h_ref.dtype)


def _out_proj(a, w, x, gain, layer, *, tm=512):
    m, k = a.shape
    n = w.shape[2]
    return pl.pallas_call(
        _outproj_kernel,
        grid=(m // tm,),
        in_specs=[
            pl.BlockSpec((tm, k), lambda i: (i, 0)),
            pl.BlockSpec((None, k, n), lambda i: (layer, 0, 0), pipeline_mode=pl.Buffered(1)),
            pl.BlockSpec((tm, n), lambda i: (i, 0)),
            pl.BlockSpec((None, 1, n), lambda i: (layer, 0, 0)),
        ],
        out_specs=[pl.BlockSpec((tm, n), lambda i: (i, 0)), pl.BlockSpec((tm, n), lambda i: (i, 0))],
        out_shape=[jax.ShapeDtypeStruct((m, n), F32), jax.ShapeDtypeStruct((m, n), BF16)],
        scratch_shapes=[pltpu.VMEM((k, n), BF16)],
        compiler_params=_params("arbitrary"),
        name="out_proj",
    )(a, w, x, gain)


def _ffn_up_kernel(h_ref, wa_ref, wb_ref, cw_ref, cb_ref, o_ref, wa_s, wb_s, a_ref, tail_ref, *, tiles_per_seq):
    tm = h_ref.shape[0]
    i = pl.program_id(1)

    @pl.when(i == 0)
    def _():
        wa_s[...] = wa_ref[...].astype(BF16)
        wb_s[...] = wb_ref[...].astype(BF16)

    seq_start = i % tiles_per_seq == 0

    @pl.when(seq_start)
    def _():
        a_ref[0:HALO_ROWS, :] = jnp.zeros((HALO_ROWS, a_ref.shape[1]), F32)

    @pl.when(jnp.logical_not(seq_start))
    def _():
        a_ref[0:HALO_ROWS, :] = tail_ref[...]

    h = h_ref[...]
    a = _dot(h, wa_s[...])
    a_ref[HALO_ROWS:, :] = a
    conv = (cb_ref[...] + cw_ref[0:1, :] * a_ref[pl.ds(HALO_ROWS - 2, tm), :]
            + cw_ref[1:2, :] * a_ref[pl.ds(HALO_ROWS - 1, tm), :] + cw_ref[2:3, :] * a)
    o_ref[...] = (_gelu(conv) * _dot(h, wb_s[...])).astype(o_ref.dtype)
    tail_ref[...] = a_ref[tm:, :]


def _ffn_up(h, w_up, conv_w, conv_b, layer, *, seq, tm=1024, tn=512):
    m, d = h.shape
    dff = w_up.shape[2] // 2
    nb = dff // tn
    return pl.pallas_call(
        functools.partial(_ffn_up_kernel, tiles_per_seq=seq // tm),
        grid=(nb, m // tm),
        in_specs=[
            pl.BlockSpec((tm, d), lambda j, i: (i, 0)),
            pl.BlockSpec((None, d, tn), lambda j, i: (layer, 0, j)),
            pl.BlockSpec((None, d, tn), lambda j, i: (layer, 0, nb + j)),
            pl.BlockSpec((None, CONV_WIDTH, tn), lambda j, i: (layer, 0, j)),
            pl.BlockSpec((None, 1, tn), lambda j, i: (layer, 0, j)),
        ],
        out_specs=pl.BlockSpec((tm, tn), lambda j, i: (i, j)),
        out_shape=jax.ShapeDtypeStruct((m, dff), BF16),
        scratch_shapes=[
            pltpu.VMEM((d, tn), BF16),
            pltpu.VMEM((d, tn), BF16),
            pltpu.VMEM((HALO_ROWS + tm, tn), F32),
            pltpu.VMEM((HALO_ROWS, tn), F32),
        ],
        compiler_params=_params("parallel", "arbitrary"),
        name="ffn_up",
    )(h, w_up, w_up, conv_w, conv_b)


def _ffn_down_kernel(a_ref, w_ref, x_ref, o_ref):
    o_ref[...] = x_ref[...] + _dot(a_ref[...], w_ref[...])


def _ffn_down(a, w, x, layer, *, tm=1024, tn=512):
    m, k = a.shape
    n = w.shape[2]
    return pl.pallas_call(
        _ffn_down_kernel,
        grid=(n // tn, m // tm),
        in_specs=[
            pl.BlockSpec((tm, k), lambda j, i: (i, 0)),
            pl.BlockSpec((None, k, tn), lambda j, i: (layer, 0, j)),
            pl.BlockSpec((tm, tn), lambda j, i: (i, j)),
        ],
        out_specs=pl.BlockSpec((tm, tn), lambda j, i: (i, j)),
        out_shape=jax.ShapeDtypeStruct((m, n), F32),
        compiler_params=_params("parallel", "parallel"),
        name="ffn_down",
    )(a, w, x)


def _norm_kernel(x_ref, g_ref, o_ref):
    o_ref[...] = _rms_scale(x_ref[...], g_ref[...])


def _final_norm(x, gain, *, tm=256):
    m, d = x.shape
    return pl.pallas_call(
        _norm_kernel,
        grid=(m // tm,),
        in_specs=[pl.BlockSpec((tm, d), lambda i: (i, 0)), pl.BlockSpec((1, d), lambda i: (0, 0))],
        out_specs=pl.BlockSpec((tm, d), lambda i: (i, 0)),
        out_shape=jax.ShapeDtypeStruct((m, d), F32),
        compiler_params=_params("parallel"),
        name="final_norm",
    )(x, gain)


def kernel(x, g_mix, w_in, b_forget, g_sgu, w_spatial, b_spatial, w_branch_a, w_branch_b, w_out, g_ffn,
           w_up, conv_w, conv_b, w_down, g_final):
    batch, seq, d = x.shape
    depth = w_in.shape[0]
    heads = b_forget.shape[1]
    fox = heads * HEAD_DIM
    sgu_w = g_sgu.shape[1]
    f0 = 3 * fox
    u0 = 3 * fox
    gate0 = u0 + 2 * sgu_w
    assert fox == sgu_w, "column-block indexing of proj assumes equal branch widths"

    w_main, w_f = _stage_w_in(jnp.swapaxes(w_in, 1, 2), f0=f0, gap=heads)
    b_f = jnp.pad(b_forget, ((0, 0), (0, LANES - heads))).reshape(depth, 1, LANES)
    w_down_b = w_down.astype(BF16)
    b_s_t = jnp.swapaxes(b_spatial, 1, 2)
    g_mix3, g_sgu3, g_ffn3 = (g.reshape(depth, 1, -1) for g in (g_mix, g_sgu, g_ffn))
    conv_b3 = conv_b.reshape(depth, 1, -1)

    xs = x.reshape(batch * seq, d)
    for l in range(depth):
        proj, logf = _in_proj(xs, g_mix3, w_main, w_f, b_f, l)
        decay = _fox_decay(logf, batch=batch, heads=heads)
        y_a = _fox_attn(proj, decay, batch=batch, heads=heads)
        merged = _merge(y_a, proj, g_sgu3, w_spatial, b_s_t, w_branch_a, w_branch_b, l, u_col=u0 // sgu_w,
                        gate_col=gate0)
        xs, h_ffn = _out_proj(merged, w_out, xs, g_ffn3, l)
        g = _ffn_up(h_ffn, w_up, conv_w, conv_b3, l, seq=seq)
        xs = _ffn_down(g, w_down_b, xs, l)
    return _final_norm(xs, g_final.reshape(1, d)).reshape(batch, seq, d)
```
